```python
import math
import jax, jax.numpy as jnp
from jax import lax
import numpy as np

D_MODEL = 1024
BATCH = 8
SEQ = 8192
DEPTH = 2
DEC_BATCH = 4
DEC_SEQ = 4096
PAST_LEN = 128

N_MIXERS = 2
N_ATTN_LAYERS = (DEPTH + 1) // 2
N_SSM_LAYERS = DEPTH // 2
HEAD_DIM = 64
N_Q_HEADS = D_MODEL // HEAD_DIM
N_KV_HEADS = 4
Q_PER_KV = N_Q_HEADS // N_KV_HEADS
WINDOW = 128
ATTN_BLOCK = 128
ROPE_THETA = 10000.0
QKV_DIM = (N_Q_HEADS + 2 * N_KV_HEADS) * HEAD_DIM
SSM_EXPAND = 2
D_INNER = SSM_EXPAND * D_MODEL
SSM_HEAD_DIM = 64
N_SSM_HEADS = D_INNER // SSM_HEAD_DIM
N_SSM_GROUPS = 4
HEADS_PER_GROUP = N_SSM_HEADS // N_SSM_GROUPS
D_STATE = 128
D_CONV = 5
CONV_PAD = D_CONV // 2
CONV_DIM = D_INNER + 2 * N_SSM_GROUPS * D_STATE
IN_PROJ_DIM = D_INNER + CONV_DIM + 2 * N_SSM_HEADS
CHUNK = 128
D_FF = 4 * D_MODEL
NORM_EPS = 1e-6
GATED_NORM_EPS = 1e-5

kernel_name = 'hybrid_swa_sink_bimamba2_sqrelu_encoder'


def rms_norm(x, w, eps=NORM_EPS):
    x32 = x.astype(jnp.float32)
    y = x32 * lax.rsqrt(jnp.mean(x32 * x32, axis=-1, keepdims=True) + eps)
    return (y * w.astype(jnp.float32)).astype(x.dtype)


def apply_rope(t, seq_len):
    inv_freq = 1.0 / (ROPE_THETA ** (jnp.arange(0, HEAD_DIM, 2, dtype=jnp.float32) / HEAD_DIM))
    ang = jnp.arange(seq_len, dtype=jnp.float32)[:, None] * inv_freq[None, :]
    cos = jnp.cos(ang)[None, :, None, :]
    sin = jnp.sin(ang)[None, :, None, :]
    t32 = t.astype(jnp.float32)
    t1, t2 = jnp.split(t32, 2, axis=-1)
    out = jnp.concatenate([t1 * cos - t2 * sin, t2 * cos + t1 * sin], axis=-1)
    return out.astype(t.dtype)


def banded_keys(t, bsz, nb):
    tp = jnp.pad(t, ((0, 0), (ATTN_BLOCK, ATTN_BLOCK), (0, 0), (0, 0)))
    tp = tp.reshape(bsz, nb + 2, ATTN_BLOCK, N_KV_HEADS, HEAD_DIM)
    return jnp.concatenate([tp[:, :-2], tp[:, 1:-1], tp[:, 2:]], axis=2)


def attn_mixer(u, w_qkv, w_o, sink):
    bsz, l, _ = u.shape
    nb = l // ATTN_BLOCK
    qkv = u @ w_qkv.astype(u.dtype)
    q = qkv[..., :N_Q_HEADS * HEAD_DIM].reshape(bsz, l, N_Q_HEADS, HEAD_DIM)
    k = qkv[..., N_Q_HEADS * HEAD_DIM:(N_Q_HEADS + N_KV_HEADS) * HEAD_DIM].reshape(bsz, l, N_KV_HEADS, HEAD_DIM)
    v = qkv[..., (N_Q_HEADS + N_KV_HEADS) * HEAD_DIM:].reshape(bsz, l, N_KV_HEADS, HEAD_DIM)
    q = apply_rope(q, l)
    k = apply_rope(k, l)
    qb = q.reshape(bsz, nb, ATTN_BLOCK, N_KV_HEADS, Q_PER_KV, HEAD_DIM)
    kb = banded_keys(k, bsz, nb)
    vb = banded_keys(v, bsz, nb)
    s = jnp.einsum('bnqkrd,bnskd->bnkrqs', qb, kb, preferred_element_type=jnp.float32) * (HEAD_DIM ** -0.5)
    qpos = jnp.arange(nb)[:, None] * ATTN_BLOCK + jnp.arange(ATTN_BLOCK)[None, :]
    kpos = jnp.arange(nb)[:, None] * ATTN_BLOCK - ATTN_BLOCK + jnp.arange(3 * ATTN_BLOCK)[None, :]
    valid = (jnp.abs(kpos[:, None, :] - qpos[:, :, None]) <= WINDOW) & (kpos >= 0)[:, None, :] & (kpos < l)[:, None, :]
    s = jnp.where(valid[None, :, None, None], s, -jnp.inf)
    sink_h = sink.astype(jnp.float32).reshape(N_KV_HEADS, Q_PER_KV)[None, None, :, :, None, None]
    m = jnp.maximum(jnp.max(s, axis=-1, keepdims=True), sink_h)
    p = jnp.exp(s - m)
    denom = jnp.sum(p, axis=-1, keepdims=True) + jnp.exp(sink_h - m)
    o = jnp.einsum('bnkrqs,bnskd->bnqkrd', (p / denom).astype(u.dtype), vb)
    return o.reshape(bsz, l, N_Q_HEADS * HEAD_DIM) @ w_o.astype(u.dtype)


def ssd_chunked(x, dt, a, b_mat, c_mat):
    bsz, l = x.shape[0], x.shape[1]
    nc = l // CHUNK
    g, r, p, n = N_SSM_GROUPS, HEADS_PER_GROUP, SSM_HEAD_DIM, D_STATE
    xs = (x * dt[..., None]).reshape(bsz, nc, CHUNK, g, r, p)
    da = (dt * a).reshape(bsz, nc, CHUNK, g, r)
    bc = b_mat.reshape(bsz, nc, CHUNK, g, n)
    cc = c_mat.reshape(bsz, nc, CHUNK, g, n)
    cum = jnp.cumsum(da, axis=2)
    tril = jnp.tril(jnp.ones((CHUNK, CHUNK), dtype=bool))
    diff = cum[:, :, :, None] - cum[:, :, None, :]
    decay = jnp.exp(jnp.where(tril[:, :, None, None], diff, -jnp.inf))
    cb = jnp.einsum('bclgn,bcsgn->bclsg', cc, bc)
    y_diag = jnp.einsum('bclsgr,bcsgrp->bclgrp', cb[..., None] * decay, xs)
    decay_states = jnp.exp(cum[:, :, -1:] - cum)
    states = jnp.einsum('bclgn,bclgrp->bcgrpn', bc, xs * decay_states[..., None])
    chunk_decay = jnp.exp(cum[:, :, -1])

    def step(h, inp):
        st, dec = inp
        return h * dec[..., None, None] + st, h

    h0 = jnp.zeros((bsz, g, r, p, n), jnp.float32)
    _, prev = lax.scan(step, h0, (jnp.moveaxis(states, 1, 0), jnp.moveaxis(chunk_decay, 1, 0)))
    prev = jnp.moveaxis(prev, 0, 1)
    y_off = jnp.einsum('bclgn,bcgrpn->bclgrp', cc, prev) * jnp.exp(cum)[..., None]
    return (y_diag + y_off).reshape(bsz, l, N_SSM_HEADS, p)


def depthwise_conv(x, w, b):
    y = lax.conv_general_dilated(x, w.astype(x.dtype)[:, None, :], window_strides=(1,),
                                 padding=[(CONV_PAD, CONV_PAD)],
                                 dimension_numbers=('NWC', 'WIO', 'NWC'),
                                 feature_group_count=x.shape[-1])
    return y + b.astype(x.dtype)


def ssm_mixer(u, w_in, conv_w, conv_b, dt_bias, a_log, d_skip, norm_w, w_out):
    bsz, l, _ = u.shape
    proj = u @ w_in.astype(u.dtype)
    z = proj[..., :D_INNER]
    xbc = proj[..., D_INNER:D_INNER + CONV_DIM]
    dt_raw = proj[..., D_INNER + CONV_DIM:]
    xbc = jax.nn.silu(depthwise_conv(xbc, conv_w, conv_b)).astype(jnp.float32)
    xh = xbc[..., :D_INNER].reshape(bsz, l, N_SSM_HEADS, SSM_HEAD_DIM)
    bm = xbc[..., D_INNER:D_INNER + N_SSM_GROUPS * D_STATE].reshape(bsz, l, N_SSM_GROUPS, D_STATE)
    cm = xbc[..., D_INNER + N_SSM_GROUPS * D_STATE:].reshape(bsz, l, N_SSM_GROUPS, D_STATE)
    dt = jax.nn.softplus(dt_raw.astype(jnp.float32).reshape(bsz, l, 2, N_SSM_HEADS) + dt_bias.astype(jnp.float32))
    a = -jnp.exp(a_log.astype(jnp.float32))
    y_fwd = ssd_chunked(xh, dt[:, :, 0], a[0], bm, cm)
    flip = lambda t: jnp.flip(t, axis=1)
    y_bwd = flip(ssd_chunked(flip(xh), flip(dt[:, :, 1]), a[1], flip(bm), flip(cm)))
    y = y_fwd + y_bwd + d_skip.astype(jnp.float32)[:, None] * xh
    y = y.reshape(bsz, l, D_INNER) * jax.nn.silu(z.astype(jnp.float32))
    y = rms_norm(y, norm_w, GATED_NORM_EPS)
    return y.astype(u.dtype) @ w_out.astype(u.dtype)


def trunk(x, attn_w_qkv, attn_w_o, attn_sink, ssm_w_in, ssm_conv_w, ssm_conv_b, ssm_dt_bias,
          ssm_a_log, ssm_d, ssm_norm_w, ssm_w_out, norm_mix_pre, norm_mix_post,
          norm_ffn_pre, norm_ffn_post, mlp_w_up, mlp_w_down):
    for i in range(DEPTH):
        j = i // N_MIXERS
        h = rms_norm(x, norm_mix_pre[i])
        if i % N_MIXERS == 0:
            h = attn_mixer(h, attn_w_qkv[j], attn_w_o[j], attn_sink[j])
        else:
            h = ssm_mixer(h, ssm_w_in[j], ssm_conv_w[j], ssm_conv_b[j], ssm_dt_bias[j],
                          ssm_a_log[j], ssm_d[j], ssm_norm_w[j], ssm_w_out[j])
        x = x + rms_norm(h, norm_mix_post[i])
        h = rms_norm(x, norm_ffn_pre[i])
        h = jnp.square(jax.nn.relu(h @ mlp_w_up[i].astype(x.dtype))) @ mlp_w_down[i].astype(x.dtype)
        x = x + rms_norm(h, norm_ffn_post[i])
    return x


def setup_inputs(seed: int = 0) -> dict:
    key = jax.random.key(seed)
    ks = jax.random.split(key, 20)
    f32 = jnp.float32

    def nrm(k, shape, scale):
        return jax.random.normal(k, shape, f32) * scale

    dt0 = jnp.exp(jax.random.uniform(ks[9], (N_SSM_LAYERS, 2, N_SSM_HEADS), f32,
                                     minval=math.log(1e-3), maxval=math.log(1e-1)))
    return {
        'x_prompt': nrm(ks[0], (BATCH, SEQ, D_MODEL), 1.0),
        'x_sample': nrm(ks[1], (DEC_BATCH, DEC_SEQ, D_MODEL), 1.0),
        'attn_w_qkv': nrm(ks[2], (N_ATTN_LAYERS, D_MODEL, QKV_DIM), D_MODEL ** -0.5),
        'attn_w_o': nrm(ks[3], (N_ATTN_LAYERS, N_Q_HEADS * HEAD_DIM, D_MODEL), (N_Q_HEADS * HEAD_DIM) ** -0.5),
        'attn_sink': nrm(ks[4], (N_ATTN_LAYERS, N_Q_HEADS), 0.5),
        'ssm_w_in': nrm(ks[5], (N_SSM_LAYERS, D_MODEL, IN_PROJ_DIM), D_MODEL ** -0.5),
        'ssm_conv_w': nrm(ks[6], (N_SSM_LAYERS, D_CONV, CONV_DIM), D_CONV ** -0.5),
        'ssm_conv_b': nrm(ks[7], (N_SSM_LAYERS, CONV_DIM), 0.01),
        'ssm_dt_bias': dt0 + jnp.log(-jnp.expm1(-dt0)),
        'ssm_a_log': jnp.log(jax.random.uniform(ks[10], (N_SSM_LAYERS, 2, N_SSM_HEADS), f32, minval=1.0, maxval=16.0)),
        'ssm_d': 1.0 + nrm(ks[11], (N_SSM_LAYERS, N_SSM_HEADS), 0.1),
        'ssm_norm_w': 1.0 + nrm(ks[12], (N_SSM_LAYERS, D_INNER), 0.1),
        'ssm_w_out': nrm(ks[13], (N_SSM_LAYERS, D_INNER, D_MODEL), D_INNER ** -0.5),
        'norm_mix_pre': 1.0 + nrm(ks[14], (DEPTH, D_MODEL), 0.1),
        'norm_mix_post': 1.0 + nrm(ks[15], (DEPTH, D_MODEL), 0.1),
        'norm_ffn_pre': 1.0 + nrm(ks[16], (DEPTH, D_MODEL), 0.1),
        'norm_ffn_post': 1.0 + nrm(ks[17], (DEPTH, D_MODEL), 0.1),
        'mlp_w_up': nrm(ks[18], (DEPTH, D_MODEL, D_FF), D_MODEL ** -0.5),
        'mlp_w_down': nrm(ks[19], (DEPTH, D_FF, D_MODEL), D_FF ** -0.5),
    }


def reference(x_prompt, x_sample, attn_w_qkv, attn_w_o, attn_sink, ssm_w_in, ssm_conv_w,
              ssm_conv_b, ssm_dt_bias, ssm_a_log, ssm_d, ssm_norm_w, ssm_w_out,
              norm_mix_pre, norm_mix_post, norm_ffn_pre, norm_ffn_post, mlp_w_up, mlp_w_down):
    weights = (attn_w_qkv, attn_w_o, attn_sink, ssm_w_in, ssm_conv_w, ssm_conv_b, ssm_dt_bias,
               ssm_a_log, ssm_d, ssm_norm_w, ssm_w_out, norm_mix_pre, norm_mix_post,
               norm_ffn_pre, norm_ffn_post, mlp_w_up, mlp_w_down)
    y_prompt = trunk(x_prompt, *weights)
    y_sample = trunk(x_sample, *weights)
    return (y_prompt, y_sample)
```

```python
import functools
import math

import jax
import jax.numpy as jnp
from jax import lax
from jax.experimental import pallas as pl
from jax.experimental.pallas import tpu as pltpu

F32 = jnp.float32
BF16 = jnp.bfloat16

D_MODEL = 1024
HEAD_DIM = 64
N_Q_HEADS = 16
N_KV_HEADS = 4
Q_PER_KV = 4
WINDOW = 128
BLK = 128
ROPE_THETA = 10000.0
D_INNER = 2048
N_SSM_HEADS = 32
N_SSM_GROUPS = 4
D_STATE = 128
D_CONV = 5
CONV_PAD = 2
CONV_DIM = D_INNER + 2 * N_SSM_GROUPS * D_STATE
GROUP_COLS = D_INNER // N_SSM_GROUPS
D_FF = 4096
NORM_EPS = 1e-6
GATED_NORM_EPS = 1e-5
LOG2E = 1.4426950408889634
NEG_BIG = -1e30

VMEM_LIMIT = 56 * 1024 * 1024


def _cparams(sem):
    return pltpu.CompilerParams(dimension_semantics=sem, vmem_limit_bytes=VMEM_LIMIT)


def _vmem_spec():
    return pl.BlockSpec(memory_space=pltpu.VMEM)


def _rms(x, w, eps):
    ms = jnp.mean(x * x, axis=-1, keepdims=True)
    return x * lax.rsqrt(ms + eps) * w


def _dot(a, b):
    return jnp.dot(a, b, preferred_element_type=F32)


def _dot_nt(a, b):
    return lax.dot_general(a, b, (((1,), (1,)), ((), ())), preferred_element_type=F32)


def _dot_tn(a, b):
    return lax.dot_general(a, b, (((0,), (0,)), ((), ())), preferred_element_type=F32)


QKV_EXT = 2048


def _qkv_kernel(x_ref, nw_ref, w_ref, cos_ref, sin_ref, q_ref, k_ref, v_ref):
    h = _rms(x_ref[...], nw_ref[...], NORM_EPS).astype(BF16)
    cos = cos_ref[...]
    sin = sin_ref[...]
    lane = lax.broadcasted_iota(jnp.int32, cos.shape, 1)
    first_half = (lane & 32) == 0
    for c in range(QKV_EXT // 256):
        y = _dot(h, w_ref[:, c * 256:(c + 1) * 256])
        for s in range(2):
            col = c * 256 + s * 128
            yc = y[:, s * 128:(s + 1) * 128]
            if col < 1536:
                rot = jnp.where(first_half, pltpu.roll(yc, 96, 1), pltpu.roll(yc, 32, 1))
                oc = yc * cos + rot * sin
                if col < 1024:
                    q_ref[:, col:col + 128] = (oc * (HEAD_DIM ** -0.5 * LOG2E)).astype(BF16)
                else:
                    k_ref[:, col - 1024:col - 896] = oc.astype(BF16)
            else:
                v_ref[:, col - 1536:col - 1408] = yc.astype(BF16)


def _qkv_call(x, nw, w_ext, cos_t, sin_t, tm):
    b, l, d = x.shape
    grid = (b, l // tm)
    row = lambda n: pl.BlockSpec((None, tm, n), lambda bi, i: (bi, i, 0))
    return pl.pallas_call(
        _qkv_kernel,
        grid=grid,
        in_specs=[row(d),
                  pl.BlockSpec((1, d), lambda bi, i: (0, 0)),
                  _vmem_spec(),
                  pl.BlockSpec((tm, 128), lambda bi, i: (i, 0)),
                  pl.BlockSpec((tm, 128), lambda bi, i: (i, 0))],
        out_specs=[row(1024), row(512), row(512)],
        out_shape=[jax.ShapeDtypeStruct((b, l, 1024), BF16),
                   jax.ShapeDtypeStruct((b, l, 512), BF16),
                   jax.ShapeDtypeStruct((b, l, 512), BF16)],
        compiler_params=_cparams(("parallel", "parallel")),
        name="qkv_rope",
    )(x, nw, w_ext, cos_t, sin_t)


def _attn_kernel(sink_ref, q_ref, kp_ref, ko_ref, kn_ref, vp_ref, vo_ref, vn_ref, x_ref,
                 wo_ref, pw_ref, out_ref, kbuf, vbuf, obuf, *, tq):
    i = pl.program_id(1)
    nsteps = pl.num_programs(1)
    nsub = tq // BLK
    kbuf[0:BLK, :] = kp_ref[...]
    kbuf[BLK:BLK + tq, :] = ko_ref[...]
    kbuf[BLK + tq:, :] = kn_ref[...]
    vbuf[0:BLK, :] = vp_ref[...]
    vbuf[BLK:BLK + tq, :] = vo_ref[...]
    vbuf[BLK + tq:, :] = vn_ref[...]

    r = lax.broadcasted_iota(jnp.int32, (BLK, 3 * BLK), 0)
    c = lax.broadcasted_iota(jnp.int32, (BLK, 3 * BLK), 1)
    lane = lax.broadcasted_iota(jnp.int32, (BLK, BLK), 1)
    lo = lane < HEAD_DIM
    zero = jnp.zeros((BLK, BLK), BF16)
    first_lo = jnp.where(i > 0, 0, BLK)
    last_hi = jnp.where(i < nsteps - 1, 3 * BLK, 2 * BLK)

    for j in range(nsub):
        cmin = r
        cmax = r + 2 * WINDOW
        if j == 0:
            cmin = jnp.maximum(cmin, first_lo)
        if j == nsub - 1:
            cmax = jnp.minimum(cmax, last_hi - 1)
        bias = jnp.where(c >= cmin, jnp.where(c <= cmax, 0.0, NEG_BIG), NEG_BIG).astype(F32)
        bias4 = jnp.concatenate([bias] * 4, axis=0)
        rows = slice(j * BLK, (j + 1) * BLK)
        for g in range(N_KV_HEADS):
            qa = q_ref[rows, g * 256:g * 256 + 128]
            qb = q_ref[rows, g * 256 + 128:g * 256 + 256]
            lhs = jnp.concatenate([jnp.where(lo, qa, zero), jnp.where(lo, zero, qa),
                                   jnp.where(lo, qb, zero), jnp.where(lo, zero, qb)], axis=0)
            kk = kbuf[j * BLK:(j + 3) * BLK, g * 128:(g + 1) * 128]
            s = _dot_nt(lhs, kk) + bias4
            sink = jnp.concatenate(
                [jnp.full((BLK, 1), sink_ref[g * Q_PER_KV + t] * LOG2E, F32) for t in range(4)], axis=0)
            m = jnp.maximum(jnp.max(s, axis=-1, keepdims=True), sink)
            p = jnp.exp2(s - m)
            denom = jnp.sum(p, axis=-1, keepdims=True) + jnp.exp2(sink - m)
            vv = vbuf[j * BLK:(j + 3) * BLK, g * 128:(g + 1) * 128]
            o = _dot(p.astype(BF16), vv) * (1.0 / denom)
            oa = jnp.where(lo, o[0:BLK], o[BLK:2 * BLK])
            ob = jnp.where(lo, o[2 * BLK:3 * BLK], o[3 * BLK:4 * BLK])
            obuf[rows, g * 256:g * 256 + 128] = oa.astype(BF16)
            obuf[rows, g * 256 + 128:g * 256 + 256] = ob.astype(BF16)
    a = _dot(obuf[...], wo_ref[...])
    out_ref[...] = x_ref[...] + _rms(a, pw_ref[...], NORM_EPS)


def _attn_call(sink, q, kd, vd, x, wo, pw, tq):
    b, l, d = x.shape
    nb = l // BLK
    rr = tq // BLK
    grid = (b, l // tq)
    own = lambda n: pl.BlockSpec((None, tq, n), lambda bi, i: (bi, i, 0))
    prev = pl.BlockSpec((None, BLK, 512), lambda bi, i: (bi, jnp.maximum(i * rr - 1, 0), 0))
    nxt = pl.BlockSpec((None, BLK, 512), lambda bi, i: (bi, jnp.minimum((i + 1) * rr, nb - 1), 0))
    return pl.pallas_call(
        functools.partial(_attn_kernel, tq=tq),
        grid=grid,
        in_specs=[pl.BlockSpec(memory_space=pltpu.SMEM),
                  own(1024), prev, own(512), nxt, prev, own(512), nxt, own(d),
                  _vmem_spec(),
                  pl.BlockSpec((1, d), lambda bi, i: (0, 0))],
        out_specs=own(d),
        out_shape=jax.ShapeDtypeStruct((b, l, d), F32),
        scratch_shapes=[pltpu.VMEM((tq + 2 * BLK, 512), BF16),
                        pltpu.VMEM((tq + 2 * BLK, 512), BF16),
                        pltpu.VMEM((tq, 1024), BF16)],
        compiler_params=_cparams(("parallel", "parallel")),
        name="attn_block",
    )(sink, q, kd, kd, kd, vd, vd, vd, x, wo, pw)


FF_CHUNK = 1024


def _ffn_kernel(x_ref, nw_ref, wu_ref, wd_ref, pw_ref, out_ref):
    x = x_ref[...]
    h = _rms(x, nw_ref[...], NORM_EPS).astype(BF16)
    acc = None
    for c in range(D_FF // FF_CHUNK):
        u = _dot(h, wu_ref[:, c * FF_CHUNK:(c + 1) * FF_CHUNK])
        a = jnp.square(jnp.maximum(u, 0.0)).astype(BF16)
        part = _dot(a, wd_ref[c * FF_CHUNK:(c + 1) * FF_CHUNK, :])
        acc = part if acc is None else acc + part
    out_ref[...] = x + _rms(acc, pw_ref[...], NORM_EPS)


def _ffn_call(x2d, nw, wu, wd, pw, tm):
    t, d = x2d.shape
    vec = pl.BlockSpec((1, d), lambda i: (0, 0))
    return pl.pallas_call(
        _ffn_kernel,
        grid=(t // tm,),
        in_specs=[pl.BlockSpec((tm, d), lambda i: (i, 0)), vec, _vmem_spec(), _vmem_spec(), vec],
        out_specs=pl.BlockSpec((tm, d), lambda i: (i, 0)),
        out_shape=jax.ShapeDtypeStruct((t, d), F32),
        compiler_params=_cparams(("parallel",)),
        name="ffn",
    )(x2d, nw, wu, wd, pw)


DT_COLS = 128
IN_EXT = D_INNER + CONV_DIM + DT_COLS


def _inproj_kernel(x_ref, nw_ref, w_ref, z_ref, xbc_ref, dt_ref):
    h = _rms(x_ref[...], nw_ref[...], NORM_EPS).astype(BF16)
    for c in range(D_INNER // 512):
        z_ref[:, c * 512:(c + 1) * 512] = _dot(h, w_ref[:, c * 512:(c + 1) * 512]).astype(BF16)
    for c in range(CONV_DIM // 512):
        lo = D_INNER + c * 512
        xbc_ref[:, c * 512:(c + 1) * 512] = _dot(h, w_ref[:, lo:lo + 512]).astype(BF16)
    dt_ref[...] = _dot(h, w_ref[:, D_INNER + CONV_DIM:])


def _inproj_call(x2d, nw, w_ext, tm):
    t, d = x2d.shape
    row = lambda n: pl.BlockSpec((tm, n), lambda i: (i, 0))
    return pl.pallas_call(
        _inproj_kernel,
        grid=(t // tm,),
        in_specs=[row(d), pl.BlockSpec((1, d), lambda i: (0, 0)), _vmem_spec()],
        out_specs=[row(D_INNER), row(CONV_DIM), row(DT_COLS)],
        out_shape=[jax.ShapeDtypeStruct((t, D_INNER), BF16),
                   jax.ShapeDtypeStruct((t, CONV_DIM), BF16),
                   jax.ShapeDtypeStruct((t, DT_COLS), F32)],
        compiler_params=_cparams(("parallel",)),
        name="in_proj",
    )(x2d, nw, w_ext)


def _softplus(x):
    return jnp.maximum(x, 0.0) + jnp.log1p(jnp.exp(-jnp.abs(x)))


def _split2(v):
    hi = v.astype(BF16)
    lo = (v - hi.astype(F32)).astype(BF16)
    return hi, lo


def _tri_cumsum(tri, v):
    hi = v.astype(BF16)
    r1 = v - hi.astype(F32)
    mid = r1.astype(BF16)
    lo = (r1 - mid.astype(F32)).astype(BF16)
    return _dot(tri, hi) + _dot(tri, mid) + _dot(tri, lo)


def _expand(v, ee_ref):
    hi, lo = _split2(v)
    return _dot(jnp.concatenate([hi, lo], axis=1), ee_ref[...])


def _dt_and_decay(dtraw_ref, rows, bias_ref, alog_ref):
    dt = _softplus(dtraw_ref[rows, :] + bias_ref[...])
    a = -jnp.exp(alog_ref[...]) * amask_row()
    return dt, dt * a


def amask_row():
    lane = lax.broadcasted_iota(jnp.int32, (1, DT_COLS), 1)
    return jnp.where(lane < 2 * N_SSM_HEADS, 1.0, 0.0).astype(F32)


def _tri(lower):
    r = lax.broadcasted_iota(jnp.int32, (BLK, BLK), 0)
    c = lax.broadcasted_iota(jnp.int32, (BLK, BLK), 1)
    m = (c <= r) if lower else (c >= r)
    return jnp.where(m, 1.0, 0.0).astype(BF16)


def _ssd_bwd_kernel(xo_ref, xp_ref, xn_ref, dtraw_ref, cw_ref, cb_ref, bias_ref, alog_ref, eeb_ref,
                    xc_ref, yoff_ref, hstate, cbuf):
    i = pl.program_id(1)
    nsteps = pl.num_programs(1)

    @pl.when(i == 0)
    def _():
        hstate[...] = jnp.zeros_like(hstate)

    has_prev = jnp.where(i < nsteps - 1, 1.0, 0.0)
    has_next = jnp.where(i > 0, 1.0, 0.0)
    cbuf[0:8, :] = xp_ref[...].astype(F32) * has_prev
    cbuf[8:8 + BLK, :] = xo_ref[...].astype(F32)
    cbuf[8 + BLK:, :] = xn_ref[...].astype(F32) * has_next
    for cblk in range(CONV_DIM // 512):
        cols = slice(cblk * 512, (cblk + 1) * 512)
        acc = cb_ref[:, cols] + cw_ref[0:1, cols] * cbuf[8 - CONV_PAD:8 - CONV_PAD + BLK, cols]
        for k in range(1, D_CONV):
            acc = acc + cw_ref[k:k + 1, cols] * cbuf[8 - CONV_PAD + k:8 - CONV_PAD + k + BLK, cols]
        xc_ref[:, cols] = (acc * (1.0 / (1.0 + jnp.exp(-acc)))).astype(BF16)

    dt, da = _dt_and_decay(dtraw_ref, slice(None), bias_ref, alog_ref)
    rb = _tri_cumsum(_tri(False), da)
    tot = rb[0:1, :]
    e_rb = _expand(jnp.exp(rb), eeb_ref)
    sc = _expand(dt * jnp.exp(tot - rb), eeb_ref)
    for g in range(N_SSM_GROUPS):
        gc = slice(g * GROUP_COLS, (g + 1) * GROUP_COLS)
        bm = xc_ref[:, D_INNER + g * D_STATE:D_INNER + (g + 1) * D_STATE]
        cm = xc_ref[:, D_INNER + 512 + g * D_STATE:D_INNER + 512 + (g + 1) * D_STATE]
        hg = hstate[g]
        yoff_ref[:, gc] = (_dot(cm, hg.astype(BF16)) * e_rb[:, gc]).astype(BF16)
        xs = (xc_ref[:, gc].astype(F32) * sc[:, gc]).astype(BF16)
        hstate[g] = hg * e_rb[0:1, gc] + _dot_tn(bm, xs)


def _ssd_bwd_call(xbc, dtraw, cw, cb, bias, alog, eeb):
    b, l, _ = xbc.shape
    nc = l // BLK
    n8 = l // 8
    blk = lambda i: nc - 1 - i
    own = lambda n: pl.BlockSpec((None, BLK, n), lambda bi, i: (bi, blk(i), 0))
    prev8 = pl.BlockSpec((None, 8, CONV_DIM), lambda bi, i: (bi, jnp.maximum(blk(i) * 16 - 1, 0), 0))
    next8 = pl.BlockSpec((None, 8, CONV_DIM), lambda bi, i: (bi, jnp.minimum((blk(i) + 1) * 16, n8 - 1), 0))
    full = lambda a: pl.BlockSpec(a.shape, lambda bi, i: (0,) * a.ndim)
    return pl.pallas_call(
        _ssd_bwd_kernel,
        grid=(b, nc),
        in_specs=[own(CONV_DIM), prev8, next8, own(DT_COLS), full(cw), full(cb), full(bias), full(alog),
                  _vmem_spec()],
        out_specs=[own(CONV_DIM), own(D_INNER)],
        out_shape=[jax.ShapeDtypeStruct((b, l, CONV_DIM), BF16),
                   jax.ShapeDtypeStruct((b, l, D_INNER), BF16)],
        scratch_shapes=[pltpu.VMEM((N_SSM_GROUPS, D_STATE, GROUP_COLS), F32),
                        pltpu.VMEM((BLK + 16, CONV_DIM), F32)],
        compiler_params=_cparams(("parallel", "arbitrary")),
        name="ssd_bwd",
    )(xbc, xbc, xbc, dtraw, cw, cb, bias, alog, eeb)


def _ssd_fwd_kernel(xc_ref, dtraw_ref, z_ref, yoffb_ref, xres_ref, bias_ref, alog_ref, dskip_ref,
                    gnw_ref, wout_ref, pw_ref, eef_ref, eeb_ref, out_ref, hstate, ybuf):
    i = pl.program_id(1)

    @pl.when(i == 0)
    def _():
        hstate[...] = jnp.zeros_like(hstate)

    dt, da = _dt_and_decay(dtraw_ref, slice(None), bias_ref, alog_ref)
    cf = _tri_cumsum(_tri(True), da)
    rb = _tri_cumsum(_tri(False), da)
    cf_t = cf.T
    rb_t = rb.T
    dt_t = dt.T
    tot = cf[BLK - 1:BLK, :]
    e_cf = _expand(jnp.exp(cf), eef_ref)
    sc = _expand(dt * jnp.exp(tot - cf), eef_ref)

    r = lax.broadcasted_iota(jnp.int32, (BLK, BLK), 0)
    c = lax.broadcasted_iota(jnp.int32, (BLK, BLK), 1)
    lower = c <= r
    lo = c < HEAD_DIM
    zero = jnp.zeros((BLK, BLK), BF16)

    cbd = jnp.zeros((BLK, DT_COLS), F32)
    lane = lax.broadcasted_iota(jnp.int32, (BLK, DT_COLS), 1)
    for g in range(N_SSM_GROUPS):
        bm = xc_ref[:, D_INNER + g * D_STATE:D_INNER + (g + 1) * D_STATE].astype(F32)
        cm = xc_ref[:, D_INNER + 512 + g * D_STATE:D_INNER + 512 + (g + 1) * D_STATE].astype(F32)
        d = jnp.sum(bm * cm, axis=1, keepdims=True)
        sel = (lane >= N_SSM_HEADS + 8 * g) & (lane < N_SSM_HEADS + 8 * (g + 1))
        cbd = jnp.where(sel, d, cbd)
    skip = _expand(cbd * dt, eeb_ref) + dskip_ref[...]

    for g in range(N_SSM_GROUPS):
        gc = slice(g * GROUP_COLS, (g + 1) * GROUP_COLS)
        bm = xc_ref[:, D_INNER + g * D_STATE:D_INNER + (g + 1) * D_STATE]
        cm = xc_ref[:, D_INNER + 512 + g * D_STATE:D_INNER + 512 + (g + 1) * D_STATE]
        cbm = _dot_nt(cm, bm)
        hg = hstate[g]
        xg = xc_ref[:, gc]
        ybuf[:, gc] = (_dot(cm, hg.astype(BF16)) * e_cf[:, gc]
                       + yoffb_ref[:, gc].astype(F32)
                       + xg.astype(F32) * skip[:, gc])
        xs = (xg.astype(F32) * sc[:, gc]).astype(BF16)
        hstate[g] = hg * e_cf[BLK - 1:BLK, gc] + _dot_tn(bm, xs)
        for pr in range(4):
            ms = []
            for t in range(2):
                h = g * 8 + pr * 2 + t
                hb = N_SSM_HEADS + h
                dmat = jnp.where(lower, cf[:, h:h + 1] - cf_t[h:h + 1, :],
                                 rb[:, hb:hb + 1] - rb_t[hb:hb + 1, :])
                w = jnp.exp(dmat) * jnp.where(lower, dt_t[h:h + 1, :], dt_t[hb:hb + 1, :])
                ms.append((cbm * w).astype(BF16))
            pc = slice(g * GROUP_COLS + pr * 128, g * GROUP_COLS + (pr + 1) * 128)
            xp = xc_ref[:, pc]
            rhs = jnp.concatenate([jnp.where(lo, xp, zero), jnp.where(lo, zero, xp)], axis=0)
            ybuf[:, pc] += _dot(jnp.concatenate(ms, axis=1), rhs)

    zf = z_ref[...].astype(F32)
    y = ybuf[...] * (zf * (1.0 / (1.0 + jnp.exp(-zf))))
    y = _rms(y, gnw_ref[...], GATED_NORM_EPS).astype(BF16)
    o = _dot(y, wout_ref[...])
    out_ref[...] = xres_ref[...] + _rms(o, pw_ref[...], NORM_EPS)


def _ssd_fwd_call(xc, dtraw, z, yoffb, xres, bias, alog, dskip, gnw, wout, pw, eef, eeb):
    b, l, d = xres.shape
    nc = l // BLK
    own = lambda n: pl.BlockSpec((None, BLK, n), lambda bi, i: (bi, i, 0))
    full = lambda a: pl.BlockSpec(a.shape, lambda bi, i: (0,) * a.ndim)
    return pl.pallas_call(
        _ssd_fwd_kernel,
        grid=(b, nc),
        in_specs=[own(CONV_DIM), own(DT_COLS), own(D_INNER), own(D_INNER), own(d),
                  full(bias), full(alog), full(dskip), full(gnw), _vmem_spec(), full(pw),
                  _vmem_spec(), _vmem_spec()],
        out_specs=own(d),
        out_shape=jax.ShapeDtypeStruct((b, l, d), F32),
        scratch_shapes=[pltpu.VMEM((N_SSM_GROUPS, D_STATE, GROUP_COLS), F32),
                        pltpu.VMEM((BLK, D_INNER), F32)],
        compiler_params=_cparams(("parallel", "arbitrary")),
        name="ssd_fwd",
    )(xc, dtraw, z, yoffb, xres, bias, alog, dskip, gnw, wout, pw, eef, eeb)


def _rope_tables(l):
    inv_freq = 1.0 / (ROPE_THETA ** (jnp.arange(0, HEAD_DIM, 2, dtype=F32) / HEAD_DIM))
    ang = jnp.arange(l, dtype=F32)[:, None] * inv_freq[None, :]
    cos = jnp.cos(ang)
    sin = jnp.sin(ang)
    cos_t = jnp.concatenate([cos, cos, cos, cos], axis=1)
    sin_t = jnp.concatenate([-sin, sin, -sin, sin], axis=1)
    return cos_t, sin_t


def _expand_matrix(first_col):
    rows = jnp.arange(2 * DT_COLS) % DT_COLS
    heads = jnp.arange(D_INNER) // HEAD_DIM
    return (rows[:, None] == (heads[None, :] + first_col)).astype(BF16)


def _dup_heads(w, n_heads):
    d = w.shape[0]
    w = w.reshape(d, n_heads, 1, HEAD_DIM)
    return jnp.broadcast_to(w, (d, n_heads, 2, HEAD_DIM)).reshape(d, n_heads * 2 * HEAD_DIM)


def _prep(attn_w_qkv, attn_w_o, attn_sink, ssm_w_in, ssm_conv_w, ssm_conv_b, ssm_dt_bias,
          ssm_a_log, ssm_d, ssm_norm_w, ssm_w_out, norm_mix_pre, norm_mix_post,
          norm_ffn_pre, norm_ffn_post, mlp_w_up, mlp_w_down):
    p = {}
    wq = attn_w_qkv[0]
    nq = N_Q_HEADS * HEAD_DIM
    nk = N_KV_HEADS * HEAD_DIM
    p["w_qkv"] = jnp.concatenate([wq[:, :nq], _dup_heads(wq[:, nq:nq + nk], N_KV_HEADS),
                                  _dup_heads(wq[:, nq + nk:], N_KV_HEADS)], axis=1).astype(BF16)
    p["w_o"] = attn_w_o[0].astype(BF16)
    p["sink"] = attn_sink[0].astype(F32)
    w_in = ssm_w_in[0]
    p["w_in"] = jnp.concatenate([w_in, jnp.zeros((D_MODEL, DT_COLS - 2 * N_SSM_HEADS), F32)], axis=1).astype(BF16)
    p["conv_w"] = ssm_conv_w[0].astype(F32)
    p["conv_b"] = ssm_conv_b[0].astype(F32).reshape(1, CONV_DIM)
    pad = jnp.zeros((DT_COLS - 2 * N_SSM_HEADS,), F32)
    p["dt_bias"] = jnp.concatenate([ssm_dt_bias[0].astype(F32).reshape(-1), pad]).reshape(1, DT_COLS)
    p["a_log"] = jnp.concatenate([ssm_a_log[0].astype(F32).reshape(-1), pad]).reshape(1, DT_COLS)
    p["d_skip"] = jnp.repeat(ssm_d[0].astype(F32), HEAD_DIM).reshape(1, D_INNER)
    p["gnw"] = ssm_norm_w[0].astype(F32).reshape(1, D_INNER)
    p["w_out"] = ssm_w_out[0].astype(BF16)
    p["mix_pre"] = norm_mix_pre.astype(F32).reshape(-1, 1, D_MODEL)
    p["mix_post"] = norm_mix_post.astype(F32).reshape(-1, 1, D_MODEL)
    p["ffn_pre"] = norm_ffn_pre.astype(F32).reshape(-1, 1, D_MODEL)
    p["ffn_post"] = norm_ffn_post.astype(F32).reshape(-1, 1, D_MODEL)
    p["w_up"] = mlp_w_up.astype(BF16)
    p["w_down"] = mlp_w_down.astype(BF16)
    p["eef"] = _expand_matrix(0)
    p["eeb"] = _expand_matrix(N_SSM_HEADS)
    return p


def _pick(n, pref):
    t = pref
    while n % t:
        t //= 2
    return t


def _trunk(x, p):
    b, l, d = x.shape
    t = b * l
    tm = _pick(t, 512)
    tq = _pick(l, 512)
    cos_t, sin_t = _rope_tables(l)
    q, kd, vd = _qkv_call(x, p["mix_pre"][0], p["w_qkv"], cos_t, sin_t, _pick(l, 512))
    x = _attn_call(p["sink"], q, kd, vd, x, p["w_o"], p["mix_post"][0], tq)
    x = _ffn_call(x.reshape(t, d), p["ffn_pre"][0], p["w_up"][0], p["w_down"][0], p["ffn_post"][0], tm)
    z, xbc, dtraw = _inproj_call(x, p["mix_pre"][1], p["w_in"], tm)
    xbc = xbc.reshape(b, l, CONV_DIM)
    dtraw = dtraw.reshape(b, l, DT_COLS)
    xc, yoffb = _ssd_bwd_call(xbc, dtraw, p["conv_w"], p["conv_b"], p["dt_bias"], p["a_log"], p["eeb"])
    x = _ssd_fwd_call(xc, dtraw, z.reshape(b, l, D_INNER), yoffb, x.reshape(b, l, d),
                      p["dt_bias"], p["a_log"], p["d_skip"], p["gnw"], p["w_out"], p["mix_post"][1],
                      p["eef"], p["eeb"])
    x = _ffn_call(x.reshape(t, d), p["ffn_pre"][1], p["w_up"][1], p["w_down"][1], p["ffn_post"][1], tm)
    return x.reshape(b, l, d)


def kernel(x_prompt, x_sample, attn_w_qkv, attn_w_o, attn_sink, ssm_w_in, ssm_conv_w, ssm_conv_b,
           ssm_dt_bias, ssm_a_log, ssm_d, ssm_norm_w, ssm_w_out, norm_mix_pre, norm_mix_post,
           norm_ffn_pre, norm_ffn_post, mlp_w_up, mlp_w_down):
    p = _prep(attn_w_qkv, attn_w_o, attn_sink, ssm_w_in, ssm_conv_w, ssm_conv_b, ssm_dt_bias,
              ssm_a_log, ssm_d, ssm_norm_w, ssm_w_out, norm_mix_pre, norm_mix_post,
              norm_ffn_pre, norm_ffn_post, mlp_w_up, mlp_w_down)
    return (_trunk(x_prompt, p), _trunk(x_sample, p))
```

```python
import functools
import math

import jax
import jax.numpy as jnp
from jax import lax
from jax.experimental import pallas as pl
from jax.experimental.pallas import tpu as pltpu

F32 = jnp.float32
BF16 = jnp.bfloat16

D_MODEL = 1024
HEAD_DIM = 64
N_Q_HEADS = 16
N_KV_HEADS = 4
Q_PER_KV = 4
WINDOW = 128
BLK = 128
ROPE_THETA = 10000.0
D_INNER = 2048
N_SSM_HEADS = 32
N_SSM_GROUPS = 4
D_STATE = 128
D_CONV = 5
CONV_PAD = 2
CONV_DIM = D_INNER + 2 * N_SSM_GROUPS * D_STATE
GROUP_COLS = D_INNER // N_SSM_GROUPS
D_FF = 4096
NORM_EPS = 1e-6
GATED_NORM_EPS = 1e-5
LOG2E = 1.4426950408889634
NEG_BIG = -1e30

VMEM_LIMIT = 56 * 1024 * 1024


def _cparams(sem):
    return pltpu.CompilerParams(dimension_semantics=sem, vmem_limit_bytes=VMEM_LIMIT)


def _vmem_spec():
    return pl.BlockSpec(memory_space=pltpu.VMEM)


def _rms(x, w, eps):
    ms = jnp.mean(x * x, axis=-1, keepdims=True)
    return x * lax.rsqrt(ms + eps) * w


def _dot(a, b):
    return jnp.dot(a, b, preferred_element_type=F32)


def _dot_nt(a, b):
    return lax.dot_general(a, b, (((1,), (1,)), ((), ())), preferred_element_type=F32)


def _dot_tn(a, b):
    return lax.dot_general(a, b, (((0,), (0,)), ((), ())), preferred_element_type=F32)


K_COLS = N_KV_HEADS * HEAD_DIM
HALF = HEAD_DIM // 2


def _qkv_kernel(x_ref, nw_ref, wqv_ref, wk_ref, cos_ref, sin_ref, cost_ref, sint_ref, q_ref, k_ref, v_ref):
    h = _rms(x_ref[...], nw_ref[...], NORM_EPS).astype(BF16)
    yk = _dot(h, wk_ref[...])
    cos = cos_ref[...]
    sin = sin_ref[...]
    lane = lax.broadcasted_iota(jnp.int32, cos.shape, 1)
    first_half = (lane & HALF) == 0
    for s in range(K_COLS // 128):
        yc = yk[:, s * 128:(s + 1) * 128]
        rot = jnp.where(first_half, pltpu.roll(yc, 128 - HALF, 1), pltpu.roll(yc, HALF, 1))
        k_ref[:, s * 128:(s + 1) * 128] = (yc * cos + rot * sin).astype(BF16)
    cost = cost_ref[...]
    sint = sint_ref[...]
    qscale = HEAD_DIM ** -0.5 * LOG2E
    yqv = _dot_nt(wqv_ref[...], h)
    for hd in range(N_Q_HEADS):
        t1 = yqv[hd * HEAD_DIM:hd * HEAD_DIM + HALF]
        t2 = yqv[hd * HEAD_DIM + HALF:(hd + 1) * HEAD_DIM]
        q_ref[hd * HEAD_DIM:hd * HEAD_DIM + HALF, :] = ((t1 * cost - t2 * sint) * qscale).astype(BF16)
        q_ref[hd * HEAD_DIM + HALF:(hd + 1) * HEAD_DIM, :] = ((t2 * cost + t1 * sint) * qscale).astype(BF16)
    ones = jnp.ones((HEAD_DIM, h.shape[0]), BF16)
    for g in range(N_KV_HEADS):
        lo = N_Q_HEADS * HEAD_DIM + g * HEAD_DIM
        v_ref[g * 128:g * 128 + HEAD_DIM, :] = yqv[lo:lo + HEAD_DIM].astype(BF16)
        v_ref[g * 128 + HEAD_DIM:(g + 1) * 128, :] = ones


def _qkv_call(x, nw, wqv_t, wk, cos_t, sin_t, cos_tt, sin_tt, tm):
    b, l, d = x.shape
    grid = (b, l // tm)
    tok = lambda n: pl.BlockSpec((None, tm, n), lambda bi, i: (bi, i, 0))
    chan = lambda n: pl.BlockSpec((None, n, tm), lambda bi, i: (bi, 0, i))
    return pl.pallas_call(
        _qkv_kernel,
        grid=grid,
        in_specs=[tok(d),
                  pl.BlockSpec((1, d), lambda bi, i: (0, 0)),
                  _vmem_spec(), _vmem_spec(),
                  pl.BlockSpec((tm, 128), lambda bi, i: (i, 0)),
                  pl.BlockSpec((tm, 128), lambda bi, i: (i, 0)),
                  pl.BlockSpec((HALF, tm), lambda bi, i: (0, i)),
                  pl.BlockSpec((HALF, tm), lambda bi, i: (0, i))],
        out_specs=[chan(1024), tok(K_COLS), chan(512)],
        out_shape=[jax.ShapeDtypeStruct((b, 1024, l), BF16),
                   jax.ShapeDtypeStruct((b, l, K_COLS), BF16),
                   jax.ShapeDtypeStruct((b, 512, l), BF16)],
        compiler_params=_cparams(("parallel", "parallel")),
        name="qkv_rope",
    )(x, nw, wqv_t, wk, cos_t, sin_t, cos_tt, sin_tt)


ATTN_PIPE_DEPTH = 2


def _attn_kernel(sink_ref, q_ref, kp_ref, ko_ref, kn_ref, vp_ref, vo_ref, vn_ref, x_ref,
                 wo_ref, pw_ref, out_ref, kbuf, vbuf, obuf, *, tq):
    i = pl.program_id(1)
    nsteps = pl.num_programs(1)
    nsub = tq // BLK
    kbuf[0:BLK, :] = kp_ref[...]
    kbuf[BLK:BLK + tq, :] = ko_ref[...]
    kbuf[BLK + tq:, :] = kn_ref[...]
    vbuf[:, 0:BLK] = vp_ref[...]
    vbuf[:, BLK:BLK + tq] = vo_ref[...]
    vbuf[:, BLK + tq:] = vn_ref[...]

    c = lax.broadcasted_iota(jnp.int32, (3 * BLK, BLK), 0)
    r = lax.broadcasted_iota(jnp.int32, (3 * BLK, BLK), 1)
    er = lax.broadcasted_iota(jnp.int32, (BLK, BLK), 0)
    ec = lax.broadcasted_iota(jnp.int32, (BLK, BLK), 1)
    eye = jnp.where(er == ec, 1.0, 0.0).astype(BF16)
    eye4 = jnp.concatenate([eye] * Q_PER_KV, axis=1)
    zero_half = jnp.zeros((HEAD_DIM, BLK), BF16)
    first_lo = jnp.where(i > 0, 0, BLK)
    last_hi = jnp.where(i < nsteps - 1, 3 * BLK, 2 * BLK)

    def scores(j, g, bias):
        qcols = slice(j * BLK, (j + 1) * BLK)
        kk = kbuf[j * BLK:(j + 3) * BLK, (g // 2) * 128:(g // 2 + 1) * 128]
        blocks = []
        for t in range(Q_PER_KV):
            hd = g * Q_PER_KV + t
            qh = q_ref[hd * HEAD_DIM:(hd + 1) * HEAD_DIM, qcols]
            blocks.append(jnp.concatenate([qh, zero_half] if g % 2 == 0 else [zero_half, qh], axis=0))
        rhs = jnp.concatenate([jnp.concatenate(blocks, axis=1), eye4], axis=0)
        s = _dot(jnp.concatenate([kk, bias], axis=1), rhs)
        sink = jnp.concatenate(
            [jnp.full((1, BLK), sink_ref[g * Q_PER_KV + t] * LOG2E, F32) for t in range(Q_PER_KV)], axis=1)
        m = jnp.maximum(jnp.max(s, axis=0, keepdims=True), sink)
        return s, m, sink

    def finish(j, g, s, m, sink):
        qcols = slice(j * BLK, (j + 1) * BLK)
        p = jnp.exp2(s - m).astype(BF16)
        o = _dot(vbuf[g * 128:(g + 1) * 128, j * BLK:(j + 3) * BLK], p)
        den = o[HEAD_DIM:] + jnp.exp2(sink - m)
        on = (o[:HEAD_DIM] * (1.0 / den)).astype(BF16)
        for t in range(Q_PER_KV):
            hd = g * Q_PER_KV + t
            obuf[hd * HEAD_DIM:(hd + 1) * HEAD_DIM, qcols] = on[:, t * BLK:(t + 1) * BLK]

    pending = []
    for j in range(nsub):
        cmin = r
        cmax = r + 2 * WINDOW
        if j == 0:
            cmin = jnp.maximum(cmin, first_lo)
        if j == nsub - 1:
            cmax = jnp.minimum(cmax, last_hi - 1)
        bias = jnp.where(c >= cmin, jnp.where(c <= cmax, 0.0, NEG_BIG), NEG_BIG).astype(BF16)
        for g in range(N_KV_HEADS):
            pending.append((j, g) + scores(j, g, bias))
            if len(pending) > ATTN_PIPE_DEPTH:
                finish(*pending.pop(0))
    for item in pending:
        finish(*item)
    a = _dot_tn(obuf[...], wo_ref[...])
    out_ref[...] = x_ref[...] + _rms(a, pw_ref[...], NORM_EPS)


def _attn_call(sink, q_t, k, v_t, x, wo, pw, tq):
    b, l, d = x.shape
    nb = l // BLK
    rr = tq // BLK
    grid = (b, l // tq)
    tok = lambda n: pl.BlockSpec((None, tq, n), lambda bi, i: (bi, i, 0))
    chan = lambda n: pl.BlockSpec((None, n, tq), lambda bi, i: (bi, 0, i))
    kprev = pl.BlockSpec((None, BLK, K_COLS), lambda bi, i: (bi, jnp.maximum(i * rr - 1, 0), 0))
    knext = pl.BlockSpec((None, BLK, K_COLS), lambda bi, i: (bi, jnp.minimum((i + 1) * rr, nb - 1), 0))
    vprev = pl.BlockSpec((None, 512, BLK), lambda bi, i: (bi, 0, jnp.maximum(i * rr - 1, 0)))
    vnext = pl.BlockSpec((None, 512, BLK), lambda bi, i: (bi, 0, jnp.minimum((i + 1) * rr, nb - 1)))
    return pl.pallas_call(
        functools.partial(_attn_kernel, tq=tq),
        grid=grid,
        in_specs=[pl.BlockSpec(memory_space=pltpu.SMEM),
                  chan(1024), kprev, tok(K_COLS), knext, vprev, chan(512), vnext, tok(d),
                  _vmem_spec(),
                  pl.BlockSpec((1, d), lambda bi, i: (0, 0))],
        out_specs=tok(d),
        out_shape=jax.ShapeDtypeStruct((b, l, d), F32),
        scratch_shapes=[pltpu.VMEM((tq + 2 * BLK, K_COLS), BF16),
                        pltpu.VMEM((512, tq + 2 * BLK), BF16),
                        pltpu.VMEM((1024, tq), BF16)],
        compiler_params=_cparams(("parallel", "parallel")),
        name="attn_block",
    )(sink, q_t, k, k, k, v_t, v_t, v_t, x, wo, pw)


FF_CHUNK = 1024


def _ffn_kernel(x_ref, nw_ref, wu_ref, wd_ref, pw_ref, out_ref):
    x = x_ref[...]
    h = _rms(x, nw_ref[...], NORM_EPS).astype(BF16)
    acc = None
    for c in range(D_FF // FF_CHUNK):
        u = _dot(h, wu_ref[:, c * FF_CHUNK:(c + 1) * FF_CHUNK])
        a = jnp.square(jnp.maximum(u, 0.0)).astype(BF16)
        part = _dot(a, wd_ref[c * FF_CHUNK:(c + 1) * FF_CHUNK, :])
        acc = part if acc is None else acc + part
    out_ref[...] = x + _rms(acc, pw_ref[...], NORM_EPS)


def _ffn_call(x2d, nw, wu, wd, pw, tm):
    t, d = x2d.shape
    vec = pl.BlockSpec((1, d), lambda i: (0, 0))
    return pl.pallas_call(
        _ffn_kernel,
        grid=(t // tm,),
        in_specs=[pl.BlockSpec((tm, d), lambda i: (i, 0)), vec, _vmem_spec(), _vmem_spec(), vec],
        out_specs=pl.BlockSpec((tm, d), lambda i: (i, 0)),
        out_shape=jax.ShapeDtypeStruct((t, d), F32),
        compiler_params=_cparams(("parallel",)),
        name="ffn",
    )(x2d, nw, wu, wd, pw)


DT_COLS = 128
IN_EXT = D_INNER + CONV_DIM + DT_COLS


def _inproj_kernel(x_ref, nw_ref, w_ref, z_ref, xbc_ref, dt_ref):
    h = _rms(x_ref[...], nw_ref[...], NORM_EPS).astype(BF16)
    for c in range(D_INNER // 512):
        z_ref[:, c * 512:(c + 1) * 512] = _dot(h, w_ref[:, c * 512:(c + 1) * 512]).astype(BF16)
    for c in range(CONV_DIM // 512):
        lo = D_INNER + c * 512
        xbc_ref[:, c * 512:(c + 1) * 512] = _dot(h, w_ref[:, lo:lo + 512]).astype(BF16)
    dt_ref[...] = _dot(h, w_ref[:, D_INNER + CONV_DIM:])


def _inproj_call(x2d, nw, w_ext, tm):
    t, d = x2d.shape
    row = lambda n: pl.BlockSpec((tm, n), lambda i: (i, 0))
    return pl.pallas_call(
        _inproj_kernel,
        grid=(t // tm,),
        in_specs=[row(d), pl.BlockSpec((1, d), lambda i: (0, 0)), _vmem_spec()],
        out_specs=[row(D_INNER), row(CONV_DIM), row(DT_COLS)],
        out_shape=[jax.ShapeDtypeStruct((t, D_INNER), BF16),
                   jax.ShapeDtypeStruct((t, CONV_DIM), BF16),
                   jax.ShapeDtypeStruct((t, DT_COLS), F32)],
        compiler_params=_cparams(("parallel",)),
        name="in_proj",
    )(x2d, nw, w_ext)


def _softplus(x):
    return jnp.maximum(x, 0.0) + jnp.log1p(jnp.exp(-jnp.abs(x)))


def _split2(v):
    hi = v.astype(BF16)
    lo = (v - hi.astype(F32)).astype(BF16)
    return hi, lo


def _tri_cumsum(tri, v):
    hi = v.astype(BF16)
    r1 = v - hi.astype(F32)
    mid = r1.astype(BF16)
    lo = (r1 - mid.astype(F32)).astype(BF16)
    return _dot(tri, hi) + _dot(tri, mid) + _dot(tri, lo)


def _expand(v, ee_ref):
    hi, lo = _split2(v)
    return _dot(jnp.concatenate([hi, lo], axis=1), ee_ref[...])


def _dt_and_decay(dtraw_ref, rows, bias_ref, alog_ref):
    dt = _softplus(dtraw_ref[rows, :] + bias_ref[...])
    a = -jnp.exp(alog_ref[...]) * amask_row()
    return dt, dt * a


def amask_row():
    lane = lax.broadcasted_iota(jnp.int32, (1, DT_COLS), 1)
    return jnp.where(lane < 2 * N_SSM_HEADS, 1.0, 0.0).astype(F32)


def _tri(lower):
    r = lax.broadcasted_iota(jnp.int32, (BLK, BLK), 0)
    c = lax.broadcasted_iota(jnp.int32, (BLK, BLK), 1)
    m = (c <= r) if lower else (c >= r)
    return jnp.where(m, 1.0, 0.0).astype(BF16)


CONV_STRIDE = 4


def _ssd_bwd_kernel(xo_ref, xp_ref, xn_ref, dtraw_ref, cw_ref, cb_ref, bias_ref, alog_ref, eeb_ref,
                    xc_ref, yoff_ref, hstate, cbuf, sbuf):
    i = pl.program_id(1)
    nsteps = pl.num_programs(1)

    @pl.when(i == 0)
    def _():
        hstate[...] = jnp.zeros_like(hstate)

    has_prev = jnp.where(i < nsteps - 1, 1.0, 0.0)
    has_next = jnp.where(i > 0, 1.0, 0.0)
    for cs in range(CONV_DIM // 128):
        cols = slice(cs * 128, (cs + 1) * 128)
        cbuf[cs, 0:8, :] = xp_ref[:, cols].astype(F32) * has_prev
        cbuf[cs, 8:8 + BLK, :] = xo_ref[:, cols].astype(F32)
        cbuf[cs, 8 + BLK:, :] = xn_ref[:, cols].astype(F32) * has_next
        wk = [jnp.broadcast_to(cw_ref[k:k + 1, cols], (8, 128)) for k in range(D_CONV)]
        bk = jnp.broadcast_to(cb_ref[:, cols], (8, 128))
        for base in range(0, BLK, 8 * CONV_STRIDE):
            for rr in range(CONV_STRIDE):
                row = base + rr
                acc = bk
                for k in range(D_CONV):
                    acc = acc + wk[k] * cbuf[cs, pl.ds(8 - CONV_PAD + k + row, 8, stride=CONV_STRIDE), :]
                sbuf[cs, pl.ds(row, 8, stride=CONV_STRIDE), :] = acc * (1.0 / (1.0 + jnp.exp(-acc)))
        xc_ref[:, cols] = sbuf[cs].astype(BF16)

    dt, da = _dt_and_decay(dtraw_ref, slice(None), bias_ref, alog_ref)
    rb = _tri_cumsum(_tri(False), da)
    tot = rb[0:1, :]
    e_rb = _expand(jnp.exp(rb), eeb_ref)
    sc = _expand(dt * jnp.exp(tot - rb), eeb_ref)
    for g in range(N_SSM_GROUPS):
        gc = slice(g * GROUP_COLS, (g + 1) * GROUP_COLS)
        bm = xc_ref[:, D_INNER + g * D_STATE:D_INNER + (g + 1) * D_STATE]
        cm = xc_ref[:, D_INNER + 512 + g * D_STATE:D_INNER + 512 + (g + 1) * D_STATE]
        hg = hstate[g]
        yoff_ref[:, gc] = (_dot(cm, hg.astype(BF16)) * e_rb[:, gc]).astype(BF16)
        xs = (xc_ref[:, gc].astype(F32) * sc[:, gc]).astype(BF16)
        hstate[g] = hg * e_rb[0:1, gc] + _dot_tn(bm, xs)


def _ssd_bwd_call(xbc, dtraw, cw, cb, bias, alog, eeb):
    b, l, _ = xbc.shape
    nc = l // BLK
    n8 = l // 8
    blk = lambda i: nc - 1 - i
    own = lambda n: pl.BlockSpec((None, BLK, n), lambda bi, i: (bi, blk(i), 0))
    prev8 = pl.BlockSpec((None, 8, CONV_DIM), lambda bi, i: (bi, jnp.maximum(blk(i) * 16 - 1, 0), 0))
    next8 = pl.BlockSpec((None, 8, CONV_DIM), lambda bi, i: (bi, jnp.minimum((blk(i) + 1) * 16, n8 - 1), 0))
    full = lambda a: pl.BlockSpec(a.shape, lambda bi, i: (0,) * a.ndim)
    return pl.pallas_call(
        _ssd_bwd_kernel,
        grid=(b, nc),
        in_specs=[own(CONV_DIM), prev8, next8, own(DT_COLS), full(cw), full(cb), full(bias), full(alog),
                  _vmem_spec()],
        out_specs=[own(CONV_DIM), own(D_INNER)],
        out_shape=[jax.ShapeDtypeStruct((b, l, CONV_DIM), BF16),
                   jax.ShapeDtypeStruct((b, l, D_INNER), BF16)],
        scratch_shapes=[pltpu.VMEM((N_SSM_GROUPS, D_STATE, GROUP_COLS), F32),
                        pltpu.VMEM((CONV_DIM // 128, BLK + 16, 128), F32),
                        pltpu.VMEM((CONV_DIM // 128, BLK, 128), F32)],
        compiler_params=_cparams(("parallel", "arbitrary")),
        name="ssd_bwd",
    )(xbc, xbc, xbc, dtraw, cw, cb, bias, alog, eeb)


def _ssd_fwd_kernel(xc_ref, dtraw_ref, z_ref, yoffb_ref, xres_ref, bias_ref, alog_ref, dskip_ref,
                    gnw_ref, wout_ref, pw_ref, eef_ref, eeb_ref, out_ref, hstate, ybuf):
    i = pl.program_id(1)

    @pl.when(i == 0)
    def _():
        hstate[...] = jnp.zeros_like(hstate)

    dt, da = _dt_and_decay(dtraw_ref, slice(None), bias_ref, alog_ref)
    cf = _tri_cumsum(_tri(True), da)
    rb = _tri_cumsum(_tri(False), da)
    tot = cf[BLK - 1:BLK, :]
    cf2 = cf * LOG2E
    rb2 = rb * LOG2E
    hrow = lax.broadcasted_iota(jnp.int32, (BLK, BLK), 0)
    src_t = jnp.where(hrow < N_SSM_HEADS, cf2.T, rb2.T) - jnp.log2(dt.T)
    e_cf = _expand(jnp.exp(cf), eef_ref)
    sc = _expand(dt * jnp.exp(tot - cf), eef_ref)

    r = lax.broadcasted_iota(jnp.int32, (BLK, BLK), 0)
    c = lax.broadcasted_iota(jnp.int32, (BLK, BLK), 1)
    lower = c <= r
    lo = c < HEAD_DIM
    zero = jnp.zeros((BLK, BLK), BF16)

    cbd = jnp.zeros((BLK, DT_COLS), F32)
    lane = lax.broadcasted_iota(jnp.int32, (BLK, DT_COLS), 1)
    for g in range(N_SSM_GROUPS):
        bm = xc_ref[:, D_INNER + g * D_STATE:D_INNER + (g + 1) * D_STATE].astype(F32)
        cm = xc_ref[:, D_INNER + 512 + g * D_STATE:D_INNER + 512 + (g + 1) * D_STATE].astype(F32)
        d = jnp.sum(bm * cm, axis=1, keepdims=True)
        sel = (lane >= N_SSM_HEADS + 8 * g) & (lane < N_SSM_HEADS + 8 * (g + 1))
        cbd = jnp.where(sel, d, cbd)
    skip = _expand(cbd * dt, eeb_ref) + dskip_ref[...]

    for g in range(N_SSM_GROUPS):
        gc = slice(g * GROUP_COLS, (g + 1) * GROUP_COLS)
        bm = xc_ref[:, D_INNER + g * D_STATE:D_INNER + (g + 1) * D_STATE]
        cm = xc_ref[:, D_INNER + 512 + g * D_STATE:D_INNER + 512 + (g + 1) * D_STATE]
        cbm = _dot_nt(cm, bm)
        hg = hstate[g]
        xg = xc_ref[:, gc]
        ybuf[:, gc] = (_dot(cm, hg.astype(BF16)) * e_cf[:, gc]
                       + yoffb_ref[:, gc].astype(F32)
                       + xg.astype(F32) * skip[:, gc])
        xs = (xg.astype(F32) * sc[:, gc]).astype(BF16)
        hstate[g] = hg * e_cf[BLK - 1:BLK, gc] + _dot_tn(bm, xs)
        for pr in range(4):
            ms = []
            for t in range(2):
                h = g * 8 + pr * 2 + t
                hb = N_SSM_HEADS + h
                dmat = jnp.where(lower, cf2[:, h:h + 1] - src_t[h:h + 1, :],
                                 rb2[:, hb:hb + 1] - src_t[hb:hb + 1, :])
                ms.append((cbm * jnp.exp2(dmat)).astype(BF16))
            pc = slice(g * GROUP_COLS + pr * 128, g * GROUP_COLS + (pr + 1) * 128)
            xp = xc_ref[:, pc]
            rhs = jnp.concatenate([jnp.where(lo, xp, zero), jnp.where(lo, zero, xp)], axis=0)
            ybuf[:, pc] += _dot(jnp.concatenate(ms, axis=1), rhs)

    zf = z_ref[...].astype(F32)
    y = ybuf[...] * (zf * (1.0 / (1.0 + jnp.exp(-zf))))
    y = _rms(y, gnw_ref[...], GATED_NORM_EPS).astype(BF16)
    o = _dot(y, wout_ref[...])
    out_ref[...] = xres_ref[...] + _rms(o, pw_ref[...], NORM_EPS)


def _ssd_fwd_call(xc, dtraw, z, yoffb, xres, bias, alog, dskip, gnw, wout, pw, eef, eeb):
    b, l, d = xres.shape
    nc = l // BLK
    own = lambda n: pl.BlockSpec((None, BLK, n), lambda bi, i: (bi, i, 0))
    full = lambda a: pl.BlockSpec(a.shape, lambda bi, i: (0,) * a.ndim)
    return pl.pallas_call(
        _ssd_fwd_kernel,
        grid=(b, nc),
        in_specs=[own(CONV_DIM), own(DT_COLS), own(D_INNER), own(D_INNER), own(d),
                  full(bias), full(alog), full(dskip), full(gnw), _vmem_spec(), full(pw),
                  _vmem_spec(), _vmem_spec()],
        out_specs=own(d),
        out_shape=jax.ShapeDtypeStruct((b, l, d), F32),
        scratch_shapes=[pltpu.VMEM((N_SSM_GROUPS, D_STATE, GROUP_COLS), F32),
                        pltpu.VMEM((BLK, D_INNER), F32)],
        compiler_params=_cparams(("parallel", "arbitrary")),
        name="ssd_fwd",
    )(xc, dtraw, z, yoffb, xres, bias, alog, dskip, gnw, wout, pw, eef, eeb)


def _rope_tables(l):
    inv_freq = 1.0 / (ROPE_THETA ** (jnp.arange(0, HEAD_DIM, 2, dtype=F32) / HEAD_DIM))
    ang = jnp.arange(l, dtype=F32)[:, None] * inv_freq[None, :]
    cos = jnp.cos(ang)
    sin = jnp.sin(ang)
    cos_t = jnp.concatenate([cos, cos, cos, cos], axis=1)
    sin_t = jnp.concatenate([-sin, sin, -sin, sin], axis=1)
    return cos_t, sin_t, cos.T, sin.T


def _expand_matrix(first_col):
    rows = jnp.arange(2 * DT_COLS) % DT_COLS
    heads = jnp.arange(D_INNER) // HEAD_DIM
    return (rows[:, None] == (heads[None, :] + first_col)).astype(BF16)


def _prep(attn_w_qkv, attn_w_o, attn_sink, ssm_w_in, ssm_conv_w, ssm_conv_b, ssm_dt_bias,
          ssm_a_log, ssm_d, ssm_norm_w, ssm_w_out, norm_mix_pre, norm_mix_post,
          norm_ffn_pre, norm_ffn_post, mlp_w_up, mlp_w_down):
    p = {}
    wq = attn_w_qkv[0]
    nq = N_Q_HEADS * HEAD_DIM
    nk = N_KV_HEADS * HEAD_DIM
    p["w_qv_t"] = jnp.concatenate([wq[:, :nq], wq[:, nq + nk:]], axis=1).T.astype(BF16)
    p["w_k"] = wq[:, nq:nq + nk].astype(BF16)
    p["w_o"] = attn_w_o[0].astype(BF16)
    p["sink"] = attn_sink[0].astype(F32)
    w_in = ssm_w_in[0]
    p["w_in"] = jnp.concatenate([w_in, jnp.zeros((D_MODEL, DT_COLS - 2 * N_SSM_HEADS), F32)], axis=1).astype(BF16)
    p["conv_w"] = ssm_conv_w[0].astype(F32)
    p["conv_b"] = ssm_conv_b[0].astype(F32).reshape(1, CONV_DIM)
    pad = jnp.zeros((DT_COLS - 2 * N_SSM_HEADS,), F32)
    p["dt_bias"] = jnp.concatenate([ssm_dt_bias[0].astype(F32).reshape(-1), pad]).reshape(1, DT_COLS)
    p["a_log"] = jnp.concatenate([ssm_a_log[0].astype(F32).reshape(-1), pad]).reshape(1, DT_COLS)
    p["d_skip"] = jnp.repeat(ssm_d[0].astype(F32), HEAD_DIM).reshape(1, D_INNER)
    p["gnw"] = ssm_norm_w[0].astype(F32).reshape(1, D_INNER)
    p["w_out"] = ssm_w_out[0].astype(BF16)
    p["mix_pre"] = norm_mix_pre.astype(F32).reshape(-1, 1, D_MODEL)
    p["mix_post"] = norm_mix_post.astype(F32).reshape(-1, 1, D_MODEL)
    p["ffn_pre"] = norm_ffn_pre.astype(F32).reshape(-1, 1, D_MODEL)
    p["ffn_post"] = norm_ffn_post.astype(F32).reshape(-1, 1, D_MODEL)
    p["w_up"] = mlp_w_up.astype(BF16)
    p["w_down"] = mlp_w_down.astype(BF16)
    p["eef"] = _expand_matrix(0)
    p["eeb"] = _expand_matrix(N_SSM_HEADS)
    return p


def _pick(n, pref):
    t = pref
    while n % t:
        t //= 2
    return t


def _trunk(x, p):
    b, l, d = x.shape
    t = b * l
    tm = _pick(t, 512)
    tq = _pick(l, 512)
    cos_t, sin_t, cos_tt, sin_tt = _rope_tables(l)
    q_t, k, v_t = _qkv_call(x, p["mix_pre"][0], p["w_qv_t"], p["w_k"], cos_t, sin_t, cos_tt, sin_tt,
                            _pick(l, 512))
    x = _attn_call(p["sink"], q_t, k, v_t, x, p["w_o"], p["mix_post"][0], tq)
    x = _ffn_call(x.reshape(t, d), p["ffn_pre"][0], p["w_up"][0], p["w_down"][0], p["ffn_post"][0], tm)
    z, xbc, dtraw = _inproj_call(x, p["mix_pre"][1], p["w_in"], tm)
    xbc = xbc.reshape(b, l, CONV_DIM)
    dtraw = dtraw.reshape(b, l, DT_COLS)
    xc, yoffb = _ssd_bwd_call(xbc, dtraw, p["conv_w"], p["conv_b"], p["dt_bias"], p["a_log"], p["eeb"])
    x = _ssd_fwd_call(xc, dtraw, z.reshape(b, l, D_INNER), yoffb, x.reshape(b, l, d),
                      p["dt_bias"], p["a_log"], p["d_skip"], p["gnw"], p["w_out"], p["mix_post"][1],
                      p["eef"], p["eeb"])
    x = _ffn_call(x.reshape(t, d), p["ffn_pre"][1], p["w_up"][1], p["w_down"][1], p["ffn_post"][1], tm)
    return x.reshape(b, l, d)


def kernel(x_prompt, x_sample, attn_w_qkv, attn_w_o, attn_sink, ssm_w_in, ssm_conv_w, ssm_conv_b,
           ssm_dt_bias, ssm_a_log, ssm_d, ssm_norm_w, ssm_w_out, norm_mix_pre, norm_mix_post,
           norm_ffn_pre, norm_ffn_post, mlp_w_up, mlp_w_down):
    p = _prep(attn_w_qkv, attn_w_o, attn_sink, ssm_w_in, ssm_conv_w, ssm_conv_b, ssm_dt_bias,
              ssm_a_log, ssm_d, ssm_norm_w, ssm_w_out, norm_mix_pre, norm_mix_post,
              norm_ffn_pre, norm_ffn_post, mlp_w_up, mlp_w_down)
    return (_trunk(x_prompt, p), _trunk(x_sample, p))
```

```python
import functools
import math

import jax
import jax.numpy as jnp
from jax import lax
from jax.experimental import pallas as pl
from jax.experimental.pallas import tpu as pltpu

F32 = jnp.float32
BF16 = jnp.bfloat16

D_MODEL = 1024
HEAD_DIM = 64
N_Q_HEADS = 16
N_KV_HEADS = 4
Q_PER_KV = 4
WINDOW = 128
BLK = 128
ROPE_THETA = 10000.0
D_INNER = 2048
N_SSM_HEADS = 32
N_SSM_GROUPS = 4
D_STATE = 128
D_CONV = 5
CONV_PAD = 2
CONV_DIM = D_INNER + 2 * N_SSM_GROUPS * D_STATE
GROUP_COLS = D_INNER // N_SSM_GROUPS
D_FF = 4096
NORM_EPS = 1e-6
GATED_NORM_EPS = 1e-5
LOG2E = 1.4426950408889634
NEG_BIG = -1e30

VMEM_LIMIT = 56 * 1024 * 1024


def _cparams(sem):
    return pltpu.CompilerParams(dimension_semantics=sem, vmem_limit_bytes=VMEM_LIMIT)


def _vmem_spec():
    return pl.BlockSpec(memory_space=pltpu.VMEM)


def _rms(x, w, eps):
    ms = jnp.mean(x * x, axis=-1, keepdims=True)
    return x * lax.rsqrt(ms + eps) * w


def _dot(a, b):
    return jnp.dot(a, b, preferred_element_type=F32)


def _dot_nt(a, b):
    return lax.dot_general(a, b, (((1,), (1,)), ((), ())), preferred_element_type=F32)


def _dot_tn(a, b):
    return lax.dot_general(a, b, (((0,), (0,)), ((), ())), preferred_element_type=F32)


K_COLS = N_KV_HEADS * HEAD_DIM
HALF = HEAD_DIM // 2


def _qkv_kernel(x_ref, nw_ref, wqv_ref, wk_ref, cos_ref, sin_ref, cost_ref, sint_ref, q_ref, k_ref, v_ref):
    h = _rms(x_ref[...], nw_ref[...], NORM_EPS).astype(BF16)
    yk = _dot(h, wk_ref[...])
    cos = cos_ref[...]
    sin = sin_ref[...]
    lane = lax.broadcasted_iota(jnp.int32, cos.shape, 1)
    first_half = (lane & HALF) == 0
    for s in range(K_COLS // 128):
        yc = yk[:, s * 128:(s + 1) * 128]
        rot = jnp.where(first_half, pltpu.roll(yc, 128 - HALF, 1), pltpu.roll(yc, HALF, 1))
        k_ref[:, s * 128:(s + 1) * 128] = (yc * cos + rot * sin).astype(BF16)
    cost = cost_ref[...]
    sint = sint_ref[...]
    qscale = HEAD_DIM ** -0.5 * LOG2E
    yqv = _dot_nt(wqv_ref[...], h)
    for hd in range(N_Q_HEADS):
        t1 = yqv[hd * HEAD_DIM:hd * HEAD_DIM + HALF]
        t2 = yqv[hd * HEAD_DIM + HALF:(hd + 1) * HEAD_DIM]
        q_ref[hd * HEAD_DIM:hd * HEAD_DIM + HALF, :] = ((t1 * cost - t2 * sint) * qscale).astype(BF16)
        q_ref[hd * HEAD_DIM + HALF:(hd + 1) * HEAD_DIM, :] = ((t2 * cost + t1 * sint) * qscale).astype(BF16)
    ones = jnp.ones((HEAD_DIM, h.shape[0]), BF16)
    for g in range(N_KV_HEADS):
        lo = N_Q_HEADS * HEAD_DIM + g * HEAD_DIM
        v_ref[g * 128:g * 128 + HEAD_DIM, :] = yqv[lo:lo + HEAD_DIM].astype(BF16)
        v_ref[g * 128 + HEAD_DIM:(g + 1) * 128, :] = ones


def _qkv_call(x, nw, wqv_t, wk, cos_t, sin_t, cos_tt, sin_tt, tm):
    b, l, d = x.shape
    grid = (b, l // tm)
    tok = lambda n: pl.BlockSpec((None, tm, n), lambda bi, i: (bi, i, 0))
    chan = lambda n: pl.BlockSpec((None, n, tm), lambda bi, i: (bi, 0, i))
    return pl.pallas_call(
        _qkv_kernel,
        grid=grid,
        in_specs=[tok(d),
                  pl.BlockSpec((1, d), lambda bi, i: (0, 0)),
                  _vmem_spec(), _vmem_spec(),
                  pl.BlockSpec((tm, 128), lambda bi, i: (i, 0)),
                  pl.BlockSpec((tm, 128), lambda bi, i: (i, 0)),
                  pl.BlockSpec((HALF, tm), lambda bi, i: (0, i)),
                  pl.BlockSpec((HALF, tm), lambda bi, i: (0, i))],
        out_specs=[chan(1024), tok(K_COLS), chan(512)],
        out_shape=[jax.ShapeDtypeStruct((b, 1024, l), BF16),
                   jax.ShapeDtypeStruct((b, l, K_COLS), BF16),
                   jax.ShapeDtypeStruct((b, 512, l), BF16)],
        compiler_params=_cparams(("parallel", "parallel")),
        name="qkv_rope",
    )(x, nw, wqv_t, wk, cos_t, sin_t, cos_tt, sin_tt)


ATTN_PIPE_DEPTH = 2


def _attn_kernel(sink_ref, q_ref, kp_ref, ko_ref, kn_ref, vp_ref, vo_ref, vn_ref, x_ref,
                 wo_ref, pw_ref, out_ref, kbuf, vbuf, obuf, *, tq):
    i = pl.program_id(1)
    nsteps = pl.num_programs(1)
    nsub = tq // BLK
    kbuf[0:BLK, :] = kp_ref[...]
    kbuf[BLK:BLK + tq, :] = ko_ref[...]
    kbuf[BLK + tq:, :] = kn_ref[...]
    vbuf[:, 0:BLK] = vp_ref[...]
    vbuf[:, BLK:BLK + tq] = vo_ref[...]
    vbuf[:, BLK + tq:] = vn_ref[...]

    c = lax.broadcasted_iota(jnp.int32, (3 * BLK, BLK), 0)
    r = lax.broadcasted_iota(jnp.int32, (3 * BLK, BLK), 1)
    er = lax.broadcasted_iota(jnp.int32, (BLK, BLK), 0)
    ec = lax.broadcasted_iota(jnp.int32, (BLK, BLK), 1)
    eye = jnp.where(er == ec, 1.0, 0.0).astype(BF16)
    eye4 = jnp.concatenate([eye] * Q_PER_KV, axis=1)
    zero_half = jnp.zeros((HEAD_DIM, BLK), BF16)
    first_lo = jnp.where(i > 0, 0, BLK)
    last_hi = jnp.where(i < nsteps - 1, 3 * BLK, 2 * BLK)

    def scores(j, g, bias):
        qcols = slice(j * BLK, (j + 1) * BLK)
        kk = kbuf[j * BLK:(j + 3) * BLK, (g // 2) * 128:(g // 2 + 1) * 128]
        blocks = []
        for t in range(Q_PER_KV):
            hd = g * Q_PER_KV + t
            qh = q_ref[hd * HEAD_DIM:(hd + 1) * HEAD_DIM, qcols]
            blocks.append(jnp.concatenate([qh, zero_half] if g % 2 == 0 else [zero_half, qh], axis=0))
        rhs = jnp.concatenate([jnp.concatenate(blocks, axis=1), eye4], axis=0)
        s = _dot(jnp.concatenate([kk, bias], axis=1), rhs)
        sink = jnp.concatenate(
            [jnp.full((1, BLK), sink_ref[g * Q_PER_KV + t] * LOG2E, F32) for t in range(Q_PER_KV)], axis=1)
        m = jnp.maximum(jnp.max(s, axis=0, keepdims=True), sink)
        return s, m, sink

    def finish(j, g, s, m, sink):
        qcols = slice(j * BLK, (j + 1) * BLK)
        p = jnp.exp2(s - m).astype(BF16)
        o = _dot(vbuf[g * 128:(g + 1) * 128, j * BLK:(j + 3) * BLK], p)
        den = o[HEAD_DIM:] + jnp.exp2(sink - m)
        on = (o[:HEAD_DIM] * (1.0 / den)).astype(BF16)
        for t in range(Q_PER_KV):
            hd = g * Q_PER_KV + t
            obuf[hd * HEAD_DIM:(hd + 1) * HEAD_DIM, qcols] = on[:, t * BLK:(t + 1) * BLK]

    pending = []
    for j in range(nsub):
        cmin = r
        cmax = r + 2 * WINDOW
        if j == 0:
            cmin = jnp.maximum(cmin, first_lo)
        if j == nsub - 1:
            cmax = jnp.minimum(cmax, last_hi - 1)
        bias = jnp.where(c >= cmin, jnp.where(c <= cmax, 0.0, NEG_BIG), NEG_BIG).astype(BF16)
        for g in range(N_KV_HEADS):
            pending.append((j, g) + scores(j, g, bias))
            if len(pending) > ATTN_PIPE_DEPTH:
                finish(*pending.pop(0))
    for item in pending:
        finish(*item)
    a = _dot_tn(obuf[...], wo_ref[...])
    out_ref[...] = x_ref[...] + _rms(a, pw_ref[...], NORM_EPS)


def _attn_call(sink, q_t, k, v_t, x, wo, pw, tq):
    b, l, d = x.shape
    nb = l // BLK
    rr = tq // BLK
    grid = (b, l // tq)
    tok = lambda n: pl.BlockSpec((None, tq, n), lambda bi, i: (bi, i, 0))
    chan = lambda n: pl.BlockSpec((None, n, tq), lambda bi, i: (bi, 0, i))
    kprev = pl.BlockSpec((None, BLK, K_COLS), lambda bi, i: (bi, jnp.maximum(i * rr - 1, 0), 0))
    knext = pl.BlockSpec((None, BLK, K_COLS), lambda bi, i: (bi, jnp.minimum((i + 1) * rr, nb - 1), 0))
    vprev = pl.BlockSpec((None, 512, BLK), lambda bi, i: (bi, 0, jnp.maximum(i * rr - 1, 0)))
    vnext = pl.BlockSpec((None, 512, BLK), lambda bi, i: (bi, 0, jnp.minimum((i + 1) * rr, nb - 1)))
    return pl.pallas_call(
        functools.partial(_attn_kernel, tq=tq),
        grid=grid,
        in_specs=[pl.BlockSpec(memory_space=pltpu.SMEM),
                  chan(1024), kprev, tok(K_COLS), knext, vprev, chan(512), vnext, tok(d),
                  _vmem_spec(),
                  pl.BlockSpec((1, d), lambda bi, i: (0, 0))],
        out_specs=tok(d),
        out_shape=jax.ShapeDtypeStruct((b, l, d), F32),
        scratch_shapes=[pltpu.VMEM((tq + 2 * BLK, K_COLS), BF16),
                        pltpu.VMEM((512, tq + 2 * BLK), BF16),
                        pltpu.VMEM((1024, tq), BF16)],
        compiler_params=_cparams(("parallel", "parallel")),
        name="attn_block",
    )(sink, q_t, k, k, k, v_t, v_t, v_t, x, wo, pw)


FF_CHUNK = 1024


def _ffn_kernel(x_ref, nw_ref, wu_ref, wd_ref, pw_ref, out_ref):
    x = x_ref[...]
    h = _rms(x, nw_ref[...], NORM_EPS).astype(BF16)
    acc = None
    for c in range(D_FF // FF_CHUNK):
        u = _dot(h, wu_ref[:, c * FF_CHUNK:(c + 1) * FF_CHUNK])
        a = jnp.square(jnp.maximum(u, 0.0)).astype(BF16)
        part = _dot(a, wd_ref[c * FF_CHUNK:(c + 1) * FF_CHUNK, :])
        acc = part if acc is None else acc + part
    out_ref[...] = x + _rms(acc, pw_ref[...], NORM_EPS)


def _ffn_call(x2d, nw, wu, wd, pw, tm):
    t, d = x2d.shape
    vec = pl.BlockSpec((1, d), lambda i: (0, 0))
    return pl.pallas_call(
        _ffn_kernel,
        grid=(t // tm,),
        in_specs=[pl.BlockSpec((tm, d), lambda i: (i, 0)), vec, _vmem_spec(), _vmem_spec(), vec],
        out_specs=pl.BlockSpec((tm, d), lambda i: (i, 0)),
        out_shape=jax.ShapeDtypeStruct((t, d), F32),
        compiler_params=_cparams(("parallel",)),
        name="ffn",
    )(x2d, nw, wu, wd, pw)


DT_COLS = 128
IN_EXT = D_INNER + CONV_DIM + DT_COLS


def _inproj_kernel(x_ref, nw_ref, w_ref, z_ref, xbc_ref, dt_ref):
    h = _rms(x_ref[...], nw_ref[...], NORM_EPS).astype(BF16)
    for c in range(D_INNER // 512):
        z_ref[:, c * 512:(c + 1) * 512] = _dot(h, w_ref[:, c * 512:(c + 1) * 512]).astype(BF16)
    for c in range(CONV_DIM // 512):
        lo = D_INNER + c * 512
        xbc_ref[:, c * 512:(c + 1) * 512] = _dot(h, w_ref[:, lo:lo + 512]).astype(BF16)
    dt_ref[...] = _dot(h, w_ref[:, D_INNER + CONV_DIM:])


def _inproj_call(x2d, nw, w_ext, tm):
    t, d = x2d.shape
    row = lambda n: pl.BlockSpec((tm, n), lambda i: (i, 0))
    return pl.pallas_call(
        _inproj_kernel,
        grid=(t // tm,),
        in_specs=[row(d), pl.BlockSpec((1, d), lambda i: (0, 0)), _vmem_spec()],
        out_specs=[row(D_INNER), row(CONV_DIM), row(DT_COLS)],
        out_shape=[jax.ShapeDtypeStruct((t, D_INNER), BF16),
                   jax.ShapeDtypeStruct((t, CONV_DIM), BF16),
                   jax.ShapeDtypeStruct((t, DT_COLS), F32)],
        compiler_params=_cparams(("parallel",)),
        name="in_proj",
    )(x2d, nw, w_ext)


def _softplus(x):
    return jnp.maximum(x, 0.0) + jnp.log1p(jnp.exp(-jnp.abs(x)))


def _tri_cumsum(tri, v):
    hi = v.astype(BF16)
    r1 = v - hi.astype(F32)
    mid = r1.astype(BF16)
    lo = (r1 - mid.astype(F32)).astype(BF16)
    return _dot(tri, hi) + _dot(tri, mid) + _dot(tri, lo)


def _expand_parts(vs, kinds):
    lane = lax.broadcasted_iota(jnp.int32, (BLK, DT_COLS), 1)
    first = lane < N_SSM_HEADS
    parts = []
    for v, kind in zip(vs, kinds):
        hi = v.astype(BF16).astype(F32)
        lo = v - hi
        if kind == "f":
            comb = jnp.where(first, hi, pltpu.roll(lo, N_SSM_HEADS, 1))
        else:
            comb = jnp.where(first, pltpu.roll(lo, DT_COLS - N_SSM_HEADS, 1), hi)
        parts.append(comb.astype(BF16))
    return jnp.concatenate(parts, axis=0)


def _dt_and_decay(dtraw_ref, rows, bias_ref, alog_ref):
    dt = _softplus(dtraw_ref[rows, :] + bias_ref[...])
    lane = lax.broadcasted_iota(jnp.int32, (1, DT_COLS), 1)
    a = jnp.where(lane < 2 * N_SSM_HEADS, -jnp.exp(alog_ref[...]), 0.0)
    return dt, dt * a


def _tri(lower):
    r = lax.broadcasted_iota(jnp.int32, (BLK, BLK), 0)
    c = lax.broadcasted_iota(jnp.int32, (BLK, BLK), 1)
    m = (c <= r) if lower else (c >= r)
    return jnp.where(m, 1.0, 0.0).astype(BF16)


SSD_TILE = 512
SSD_PIPE_DEPTH = 2
HALO = 16
CONV_STRIDE = 4


def _ssd_bwd_kernel(xo_ref, xp_ref, xn_ref, dtraw_ref, cw_ref, cb_ref, bias_ref, alog_ref, ee_ref,
                    xc_ref, yoff_ref, hstate, xpad, cbuf, sbuf, *, nch):
    i = pl.program_id(1)
    nsteps = pl.num_programs(1)
    ts = nch * BLK

    @pl.when(i == 0)
    def _():
        hstate[...] = jnp.zeros_like(hstate)

    zero_halo = jnp.zeros((HALO, CONV_DIM), BF16)
    xpad[0:HALO, :] = jnp.where(i < nsteps - 1, xp_ref[...], zero_halo)
    xpad[HALO:HALO + ts, :] = xo_ref[...]
    xpad[HALO + ts:, :] = jnp.where(i > 0, xn_ref[...], zero_halo)

    def chunk(it, carry):
        r0 = pl.multiple_of((nch - 1 - it) * BLK, BLK)
        rows = pl.ds(r0, BLK)
        for cs in range(CONV_DIM // 128):
            cols = slice(cs * 128, (cs + 1) * 128)
            cbuf[cs] = xpad[pl.ds(r0, BLK + 2 * HALO), cols].astype(F32)
            wk = [jnp.broadcast_to(cw_ref[k:k + 1, cols], (8, 128)) for k in range(D_CONV)]
            bk = jnp.broadcast_to(cb_ref[:, cols], (8, 128))
            for base in range(0, BLK, 8 * CONV_STRIDE):
                for rr in range(CONV_STRIDE):
                    row = base + rr
                    acc = bk
                    for k in range(D_CONV):
                        acc = acc + wk[k] * cbuf[cs, pl.ds(HALO - CONV_PAD + k + row, 8, stride=CONV_STRIDE), :]
                    sbuf[cs, pl.ds(row, 8, stride=CONV_STRIDE), :] = acc * (1.0 / (1.0 + jnp.exp(-acc)))
            xc_ref[rows, cols] = sbuf[cs].astype(BF16)

        dt, da = _dt_and_decay(dtraw_ref, rows, bias_ref, alog_ref)
        rb = _tri_cumsum(_tri(False), da)
        tot = rb[0:1, :]
        ex = _dot(_expand_parts([jnp.exp(rb), dt * jnp.exp(tot - rb)], "bb"), ee_ref[...])
        for g in range(N_SSM_GROUPS):
            gc = slice(g * GROUP_COLS, (g + 1) * GROUP_COLS)
            bm = xc_ref[rows, D_INNER + g * D_STATE:D_INNER + (g + 1) * D_STATE]
            cm = xc_ref[rows, D_INNER + 512 + g * D_STATE:D_INNER + 512 + (g + 1) * D_STATE]
            hg = hstate[g]
            yoff_ref[rows, gc] = (_dot(cm, hg.astype(BF16)) * ex[0:BLK, gc]).astype(BF16)
            xs = xc_ref[rows, gc] * ex[BLK:2 * BLK, gc].astype(BF16)
            hstate[g] = hg * ex[0:1, gc] + _dot_tn(bm, xs)
        return carry

    lax.fori_loop(0, nch, chunk, 0)


def _ssd_bwd_call(xbc, dtraw, cw, cb, bias, alog, ee):
    b, l, _ = xbc.shape
    ts = _pick(l, SSD_TILE)
    nt = l // ts
    nh = l // HALO
    hpt = ts // HALO
    tile = lambda i: nt - 1 - i
    own = lambda n: pl.BlockSpec((None, ts, n), lambda bi, i: (bi, tile(i), 0))
    prev = pl.BlockSpec((None, HALO, CONV_DIM), lambda bi, i: (bi, jnp.maximum(tile(i) * hpt - 1, 0), 0))
    nxt = pl.BlockSpec((None, HALO, CONV_DIM), lambda bi, i: (bi, jnp.minimum((tile(i) + 1) * hpt, nh - 1), 0))
    full = lambda a: pl.BlockSpec(a.shape, lambda bi, i: (0,) * a.ndim)
    return pl.pallas_call(
        functools.partial(_ssd_bwd_kernel, nch=ts // BLK),
        grid=(b, nt),
        in_specs=[own(CONV_DIM), prev, nxt, own(DT_COLS), full(cw), full(cb), full(bias), full(alog),
                  _vmem_spec()],
        out_specs=[own(CONV_DIM), own(D_INNER)],
        out_shape=[jax.ShapeDtypeStruct((b, l, CONV_DIM), BF16),
                   jax.ShapeDtypeStruct((b, l, D_INNER), BF16)],
        scratch_shapes=[pltpu.VMEM((N_SSM_GROUPS, D_STATE, GROUP_COLS), F32),
                        pltpu.VMEM((ts + 2 * HALO, CONV_DIM), BF16),
                        pltpu.VMEM((CONV_DIM // 128, BLK + 2 * HALO, 128), F32),
                        pltpu.VMEM((CONV_DIM // 128, BLK, 128), F32)],
        compiler_params=_cparams(("parallel", "arbitrary")),
        name="ssd_bwd",
    )(xbc, xbc, xbc, dtraw, cw, cb, bias, alog, ee)


def _ssd_fwd_kernel(xc_ref, dtraw_ref, z_ref, yoffb_ref, xres_ref, bias_ref, alog_ref, dskip_ref,
                    gnw_ref, wout_ref, pw_ref, ee_ref, out_ref, hstate, ybuf, *, nch):
    i = pl.program_id(1)

    @pl.when(i == 0)
    def _():
        hstate[...] = jnp.zeros_like(hstate)

    def chunk(ci):
        rows = slice(ci * BLK, (ci + 1) * BLK)
        dt, da = _dt_and_decay(dtraw_ref, rows, bias_ref, alog_ref)
        cf = _tri_cumsum(_tri(True), da)
        rb = _tri_cumsum(_tri(False), da)
        tot = cf[BLK - 1:BLK, :]
        cf2 = cf * LOG2E
        rb2 = rb * LOG2E
        r = lax.broadcasted_iota(jnp.int32, (BLK, BLK), 0)
        c = lax.broadcasted_iota(jnp.int32, (BLK, BLK), 1)
        src_t = jnp.where(r < N_SSM_HEADS, cf2.T, rb2.T) - jnp.log2(dt.T)
        lower = c <= r
        keep_lo = jnp.where(c < HEAD_DIM, 1.0, 0.0).astype(BF16)
        keep_hi = jnp.where(c < HEAD_DIM, 0.0, 1.0).astype(BF16)

        cbd = jnp.zeros((BLK, DT_COLS), F32)
        for g in range(N_SSM_GROUPS):
            bm = xc_ref[rows, D_INNER + g * D_STATE:D_INNER + (g + 1) * D_STATE].astype(F32)
            cm = xc_ref[rows, D_INNER + 512 + g * D_STATE:D_INNER + 512 + (g + 1) * D_STATE].astype(F32)
            d = jnp.sum(bm * cm, axis=1, keepdims=True)
            sel = (c >= N_SSM_HEADS + 8 * g) & (c < N_SSM_HEADS + 8 * (g + 1))
            cbd = jnp.where(sel, d, cbd)
        parts = _expand_parts([jnp.exp(cf), dt * jnp.exp(tot - cf), cbd * dt + dskip_ref[...]], "ffb")

        def pair_out(pc, ms, yr):
            xp = xc_ref[rows, pc]
            rhs = jnp.concatenate([xp * keep_lo, xp * keep_hi], axis=0)
            ybuf[rows, pc] = _dot(jnp.concatenate(ms, axis=1), rhs) + yr

        pending = []
        for g in range(N_SSM_GROUPS):
            gc = slice(g * GROUP_COLS, (g + 1) * GROUP_COLS)
            bm = xc_ref[rows, D_INNER + g * D_STATE:D_INNER + (g + 1) * D_STATE]
            cm = xc_ref[rows, D_INNER + 512 + g * D_STATE:D_INNER + 512 + (g + 1) * D_STATE]
            cbm = _dot_nt(cm, bm)
            ex = _dot(parts, ee_ref[:, gc])
            hg = hstate[g]
            xg = xc_ref[rows, gc]
            yrest = (_dot(cm, hg.astype(BF16)) * ex[0:BLK]
                     + yoffb_ref[rows, gc].astype(F32)
                     + xg.astype(F32) * ex[2 * BLK:3 * BLK])
            xs = xg * ex[BLK:2 * BLK].astype(BF16)
            hstate[g] = hg * ex[BLK - 1:BLK] + _dot_tn(bm, xs)
            for pr in range(4):
                ms = []
                for t in range(2):
                    h = g * 8 + pr * 2 + t
                    hb = N_SSM_HEADS + h
                    dmat = jnp.where(lower, cf2[:, h:h + 1] - src_t[h:h + 1, :],
                                     rb2[:, hb:hb + 1] - src_t[hb:hb + 1, :])
                    ms.append((cbm * jnp.exp2(dmat)).astype(BF16))
                pc = slice(g * GROUP_COLS + pr * 128, g * GROUP_COLS + (pr + 1) * 128)
                pending.append((pc, ms, yrest[:, pr * 128:(pr + 1) * 128]))
                if len(pending) > SSD_PIPE_DEPTH:
                    pair_out(*pending.pop(0))
        for item in pending:
            pair_out(*item)

    def finish(ci):
        rows = slice(ci * BLK, (ci + 1) * BLK)
        zf = z_ref[rows, :].astype(F32)
        y = ybuf[rows, :] * (zf * (1.0 / (1.0 + jnp.exp(-zf))))
        scale = lax.rsqrt(jnp.mean(y * y, axis=-1, keepdims=True) + GATED_NORM_EPS)
        o = _dot((y * gnw_ref[...]).astype(BF16), wout_ref[...]) * scale
        out_ref[rows, :] = xres_ref[rows, :] + _rms(o, pw_ref[...], NORM_EPS)

    for ci in range(nch):
        chunk(ci)
        if ci > 0:
            finish(ci - 1)
    finish(nch - 1)


def _ssd_fwd_call(xc, dtraw, z, yoffb, xres, bias, alog, dskip, gnw, wout, pw, ee):
    b, l, d = xres.shape
    ts = _pick(l, SSD_TILE)
    own = lambda n: pl.BlockSpec((None, ts, n), lambda bi, i: (bi, i, 0))
    full = lambda a: pl.BlockSpec(a.shape, lambda bi, i: (0,) * a.ndim)
    return pl.pallas_call(
        functools.partial(_ssd_fwd_kernel, nch=ts // BLK),
        grid=(b, l // ts),
        in_specs=[own(CONV_DIM), own(DT_COLS), own(D_INNER), own(D_INNER), own(d),
                  full(bias), full(alog), full(dskip), full(gnw), _vmem_spec(), full(pw), _vmem_spec()],
        out_specs=own(d),
        out_shape=jax.ShapeDtypeStruct((b, l, d), F32),
        scratch_shapes=[pltpu.VMEM((N_SSM_GROUPS, D_STATE, GROUP_COLS), F32),
                        pltpu.VMEM((ts, D_INNER), F32)],
        compiler_params=_cparams(("parallel", "arbitrary")),
        name="ssd_fwd",
    )(xc, dtraw, z, yoffb, xres, bias, alog, dskip, gnw, wout, pw, ee)


def _rope_tables(l):
    inv_freq = 1.0 / (ROPE_THETA ** (jnp.arange(0, HEAD_DIM, 2, dtype=F32) / HEAD_DIM))
    ang = jnp.arange(l, dtype=F32)[:, None] * inv_freq[None, :]
    cos = jnp.cos(ang)
    sin = jnp.sin(ang)
    cos_t = jnp.concatenate([cos, cos, cos, cos], axis=1)
    sin_t = jnp.concatenate([-sin, sin, -sin, sin], axis=1)
    return cos_t, sin_t, cos.T, sin.T


def _expand_matrix():
    rows = jnp.arange(DT_COLS)
    heads = jnp.arange(D_INNER) // HEAD_DIM
    hit = ((rows[:, None] % N_SSM_HEADS) == heads[None, :]) & (rows[:, None] < 2 * N_SSM_HEADS)
    return hit.astype(BF16)


def _prep(attn_w_qkv, attn_w_o, attn_sink, ssm_w_in, ssm_conv_w, ssm_conv_b, ssm_dt_bias,
          ssm_a_log, ssm_d, ssm_norm_w, ssm_w_out, norm_mix_pre, norm_mix_post,
          norm_ffn_pre, norm_ffn_post, mlp_w_up, mlp_w_down):
    p = {}
    wq = attn_w_qkv[0]
    nq = N_Q_HEADS * HEAD_DIM
    nk = N_KV_HEADS * HEAD_DIM
    p["w_qv_t"] = jnp.concatenate([wq[:, :nq], wq[:, nq + nk:]], axis=1).T.astype(BF16)
    p["w_k"] = wq[:, nq:nq + nk].astype(BF16)
    p["w_o"] = attn_w_o[0].astype(BF16)
    p["sink"] = attn_sink[0].astype(F32)
    w_in = ssm_w_in[0]
    p["w_in"] = jnp.concatenate([w_in, jnp.zeros((D_MODEL, DT_COLS - 2 * N_SSM_HEADS), F32)], axis=1).astype(BF16)
    p["conv_w"] = ssm_conv_w[0].astype(F32)
    p["conv_b"] = ssm_conv_b[0].astype(F32).reshape(1, CONV_DIM)
    pad = jnp.zeros((DT_COLS - 2 * N_SSM_HEADS,), F32)
    p["dt_bias"] = jnp.concatenate([ssm_dt_bias[0].astype(F32).reshape(-1), pad]).reshape(1, DT_COLS)
    p["a_log"] = jnp.concatenate([ssm_a_log[0].astype(F32).reshape(-1), pad]).reshape(1, DT_COLS)
    p["d_skip"] = jnp.concatenate([jnp.zeros((N_SSM_HEADS,), F32), ssm_d[0].astype(F32), pad]).reshape(1, DT_COLS)
    p["gnw"] = ssm_norm_w[0].astype(F32).reshape(1, D_INNER)
    p["w_out"] = ssm_w_out[0].astype(BF16)
    p["mix_pre"] = norm_mix_pre.astype(F32).reshape(-1, 1, D_MODEL)
    p["mix_post"] = norm_mix_post.astype(F32).reshape(-1, 1, D_MODEL)
    p["ffn_pre"] = norm_ffn_pre.astype(F32).reshape(-1, 1, D_MODEL)
    p["ffn_post"] = norm_ffn_post.astype(F32).reshape(-1, 1, D_MODEL)
    p["w_up"] = mlp_w_up.astype(BF16)
    p["w_down"] = mlp_w_down.astype(BF16)
    p["ee"] = _expand_matrix()
    return p


def _pick(n, pref):
    t = pref
    while n % t:
        t //= 2
    return t


def _trunk(x, p):
    b, l, d = x.shape
    t = b * l
    tm = _pick(t, 512)
    tq = _pick(l, 512)
    cos_t, sin_t, cos_tt, sin_tt = _rope_tables(l)
    q_t, k, v_t = _qkv_call(x, p["mix_pre"][0], p["w_qv_t"], p["w_k"], cos_t, sin_t, cos_tt, sin_tt,
                            _pick(l, 512))
    x = _attn_call(p["sink"], q_t, k, v_t, x, p["w_o"], p["mix_post"][0], tq)
    x = _ffn_call(x.reshape(t, d), p["ffn_pre"][0], p["w_up"][0], p["w_down"][0], p["ffn_post"][0], tm)
    z, xbc, dtraw = _inproj_call(x, p["mix_pre"][1], p["w_in"], tm)
    xbc = xbc.reshape(b, l, CONV_DIM)
    dtraw = dtraw.reshape(b, l, DT_COLS)
    xc, yoffb = _ssd_bwd_call(xbc, dtraw, p["conv_w"], p["conv_b"], p["dt_bias"], p["a_log"], p["ee"])
    x = _ssd_fwd_call(xc, dtraw, z.reshape(b, l, D_INNER), yoffb, x.reshape(b, l, d),
                      p["dt_bias"], p["a_log"], p["d_skip"], p["gnw"], p["w_out"], p["mix_post"][1],
                      p["ee"])
    x = _ffn_call(x.reshape(t, d), p["ffn_pre"][1], p["w_up"][1], p["w_down"][1], p["ffn_post"][1], tm)
    return x.reshape(b, l, d)


def kernel(x_prompt, x_sample, attn_w_qkv, attn_w_o, attn_sink, ssm_w_in, ssm_conv_w, ssm_conv_b,
           ssm_dt_bias, ssm_a_log, ssm_d, ssm_norm_w, ssm_w_out, norm_mix_pre, norm_mix_post,
           norm_ffn_pre, norm_ffn_post, mlp_w_up, mlp_w_down):
    p = _prep(attn_w_qkv, attn_w_o, attn_sink, ssm_w_in, ssm_conv_w, ssm_conv_b, ssm_dt_bias,
              ssm_a_log, ssm_d, ssm_norm_w, ssm_w_out, norm_mix_pre, norm_mix_post,
              norm_ffn_pre, norm_ffn_post, mlp_w_up, mlp_w_down)
    return (_trunk(x_prompt, p), _trunk(x_sample, p))
```

```python
import functools
import math

import jax
import jax.numpy as jnp
from jax import lax
from jax.experimental import pallas as pl
from jax.experimental.pallas import tpu as pltpu

F32 = jnp.float32
BF16 = jnp.bfloat16

D_MODEL = 1024
HEAD_DIM = 64
N_Q_HEADS = 16
N_KV_HEADS = 4
Q_PER_KV = 4
WINDOW = 128
BLK = 128
ROPE_THETA = 10000.0
D_INNER = 2048
N_SSM_HEADS = 32
N_SSM_GROUPS = 4
D_STATE = 128
D_CONV = 5
CONV_PAD = 2
CONV_DIM = D_INNER + 2 * N_SSM_GROUPS * D_STATE
GROUP_COLS = D_INNER // N_SSM_GROUPS
D_FF = 4096
NORM_EPS = 1e-6
GATED_NORM_EPS = 1e-5
LOG2E = 1.4426950408889634
NEG_BIG = -1e30

VMEM_LIMIT = 56 * 1024 * 1024


def _cparams(sem):
    return pltpu.CompilerParams(dimension_semantics=sem, vmem_limit_bytes=VMEM_LIMIT)


def _vmem_spec():
    return pl.BlockSpec(memory_space=pltpu.VMEM)


def _rms(x, w, eps):
    ms = jnp.mean(x * x, axis=-1, keepdims=True)
    return x * lax.rsqrt(ms + eps) * w


def _dot(a, b):
    return jnp.dot(a, b, preferred_element_type=F32)


def _dot_nt(a, b):
    return lax.dot_general(a, b, (((1,), (1,)), ((), ())), preferred_element_type=F32)


def _dot_tn(a, b):
    return lax.dot_general(a, b, (((0,), (0,)), ((), ())), preferred_element_type=F32)


K_COLS = N_KV_HEADS * HEAD_DIM
HALF = HEAD_DIM // 2


def _qkv_kernel(x_ref, nw_ref, wqv_ref, wk_ref, cos_ref, sin_ref, cost_ref, sint_ref, q_ref, k_ref, v_ref):
    h = _rms(x_ref[...], nw_ref[...], NORM_EPS).astype(BF16)
    yk = _dot(h, wk_ref[...])
    cos = cos_ref[...]
    sin = sin_ref[...]
    lane = lax.broadcasted_iota(jnp.int32, cos.shape, 1)
    first_half = (lane & HALF) == 0
    for s in range(K_COLS // 128):
        yc = yk[:, s * 128:(s + 1) * 128]
        rot = jnp.where(first_half, pltpu.roll(yc, 128 - HALF, 1), pltpu.roll(yc, HALF, 1))
        k_ref[:, s * 128:(s + 1) * 128] = (yc * cos + rot * sin).astype(BF16)
    cost = cost_ref[...]
    sint = sint_ref[...]
    qscale = HEAD_DIM ** -0.5 * LOG2E
    yqv = _dot_nt(wqv_ref[...], h)
    for hd in range(N_Q_HEADS):
        t1 = yqv[hd * HEAD_DIM:hd * HEAD_DIM + HALF]
        t2 = yqv[hd * HEAD_DIM + HALF:(hd + 1) * HEAD_DIM]
        q_ref[hd * HEAD_DIM:hd * HEAD_DIM + HALF, :] = ((t1 * cost - t2 * sint) * qscale).astype(BF16)
        q_ref[hd * HEAD_DIM + HALF:(hd + 1) * HEAD_DIM, :] = ((t2 * cost + t1 * sint) * qscale).astype(BF16)
    ones = jnp.ones((HEAD_DIM, h.shape[0]), BF16)
    for g in range(N_KV_HEADS):
        lo = N_Q_HEADS * HEAD_DIM + g * HEAD_DIM
        v_ref[g * 128:g * 128 + HEAD_DIM, :] = yqv[lo:lo + HEAD_DIM].astype(BF16)
        v_ref[g * 128 + HEAD_DIM:(g + 1) * 128, :] = ones


def _qkv_call(x, nw, wqv_t, wk, cos_t, sin_t, cos_tt, sin_tt, tm):
    b, l, d = x.shape
    grid = (b, l // tm)
    tok = lambda n: pl.BlockSpec((None, tm, n), lambda bi, i: (bi, i, 0))
    chan = lambda n: pl.BlockSpec((None, n, tm), lambda bi, i: (bi, 0, i))
    return pl.pallas_call(
        _qkv_kernel,
        grid=grid,
        in_specs=[tok(d),
                  pl.BlockSpec((1, d), lambda bi, i: (0, 0)),
                  _vmem_spec(), _vmem_spec(),
                  pl.BlockSpec((tm, 128), lambda bi, i: (i, 0)),
                  pl.BlockSpec((tm, 128), lambda bi, i: (i, 0)),
                  pl.BlockSpec((HALF, tm), lambda bi, i: (0, i)),
                  pl.BlockSpec((HALF, tm), lambda bi, i: (0, i))],
        out_specs=[chan(1024), tok(K_COLS), chan(512)],
        out_shape=[jax.ShapeDtypeStruct((b, 1024, l), BF16),
                   jax.ShapeDtypeStruct((b, l, K_COLS), BF16),
                   jax.ShapeDtypeStruct((b, 512, l), BF16)],
        compiler_params=_cparams(("parallel", "parallel")),
        name="qkv_rope",
    )(x, nw, wqv_t, wk, cos_t, sin_t, cos_tt, sin_tt)


ATTN_PIPE_DEPTH = 2


def _attn_kernel(sink_ref, q_ref, kp_ref, ko_ref, kn_ref, vp_ref, vo_ref, vn_ref, x_ref,
                 wo_ref, pw_ref, out_ref, kbuf, vbuf, obuf, *, tq):
    i = pl.program_id(1)
    nsteps = pl.num_programs(1)
    nsub = tq // BLK
    kbuf[0:BLK, :] = kp_ref[...]
    kbuf[BLK:BLK + tq, :] = ko_ref[...]
    kbuf[BLK + tq:, :] = kn_ref[...]
    vbuf[:, 0:BLK] = vp_ref[...]
    vbuf[:, BLK:BLK + tq] = vo_ref[...]
    vbuf[:, BLK + tq:] = vn_ref[...]

    c = lax.broadcasted_iota(jnp.int32, (3 * BLK, BLK), 0)
    r = lax.broadcasted_iota(jnp.int32, (3 * BLK, BLK), 1)
    er = lax.broadcasted_iota(jnp.int32, (BLK, BLK), 0)
    ec = lax.broadcasted_iota(jnp.int32, (BLK, BLK), 1)
    eye = jnp.where(er == ec, 1.0, 0.0).astype(BF16)
    eye4 = jnp.concatenate([eye] * Q_PER_KV, axis=1)
    zero_half = jnp.zeros((HEAD_DIM, BLK), BF16)
    first_lo = jnp.where(i > 0, 0, BLK)
    last_hi = jnp.where(i < nsteps - 1, 3 * BLK, 2 * BLK)

    def scores(j, g, bias):
        qcols = slice(j * BLK, (j + 1) * BLK)
        kk = kbuf[j * BLK:(j + 3) * BLK, (g // 2) * 128:(g // 2 + 1) * 128]
        blocks = []
        for t in range(Q_PER_KV):
            hd = g * Q_PER_KV + t
            qh = q_ref[hd * HEAD_DIM:(hd + 1) * HEAD_DIM, qcols]
            blocks.append(jnp.concatenate([qh, zero_half] if g % 2 == 0 else [zero_half, qh], axis=0))
        rhs = jnp.concatenate([jnp.concatenate(blocks, axis=1), eye4], axis=0)
        s = _dot(jnp.concatenate([kk, bias], axis=1), rhs)
        sink = jnp.concatenate(
            [jnp.full((1, BLK), sink_ref[g * Q_PER_KV + t] * LOG2E, F32) for t in range(Q_PER_KV)], axis=1)
        m = jnp.maximum(jnp.max(s, axis=0, keepdims=True), sink)
        return s, m, sink

    def finish(j, g, s, m, sink):
        qcols = slice(j * BLK, (j + 1) * BLK)
        p = jnp.exp2(s - m).astype(BF16)
        o = _dot(vbuf[g * 128:(g + 1) * 128, j * BLK:(j + 3) * BLK], p)
        den = o[HEAD_DIM:] + jnp.exp2(sink - m)
        on = (o[:HEAD_DIM] * (1.0 / den)).astype(BF16)
        for t in range(Q_PER_KV):
            hd = g * Q_PER_KV + t
            obuf[hd * HEAD_DIM:(hd + 1) * HEAD_DIM, qcols] = on[:, t * BLK:(t + 1) * BLK]

    pending = []
    for j in range(nsub):
        cmin = r
        cmax = r + 2 * WINDOW
        if j == 0:
            cmin = jnp.maximum(cmin, first_lo)
        if j == nsub - 1:
            cmax = jnp.minimum(cmax, last_hi - 1)
        bias = jnp.where(c >= cmin, jnp.where(c <= cmax, 0.0, NEG_BIG), NEG_BIG).astype(BF16)
        for g in range(N_KV_HEADS):
            pending.append((j, g) + scores(j, g, bias))
            if len(pending) > ATTN_PIPE_DEPTH:
                finish(*pending.pop(0))
    for item in pending:
        finish(*item)
    a = _dot_tn(obuf[...], wo_ref[...])
    out_ref[...] = x_ref[...] + _rms(a, pw_ref[...], NORM_EPS)


def _attn_call(sink, q_t, k, v_t, x, wo, pw, tq):
    b, l, d = x.shape
    nb = l // BLK
    rr = tq // BLK
    grid = (b, l // tq)
    tok = lambda n: pl.BlockSpec((None, tq, n), lambda bi, i: (bi, i, 0))
    chan = lambda n: pl.BlockSpec((None, n, tq), lambda bi, i: (bi, 0, i))
    kprev = pl.BlockSpec((None, BLK, K_COLS), lambda bi, i: (bi, jnp.maximum(i * rr - 1, 0), 0))
    knext = pl.BlockSpec((None, BLK, K_COLS), lambda bi, i: (bi, jnp.minimum((i + 1) * rr, nb - 1), 0))
    vprev = pl.BlockSpec((None, 512, BLK), lambda bi, i: (bi, 0, jnp.maximum(i * rr - 1, 0)))
    vnext = pl.BlockSpec((None, 512, BLK), lambda bi, i: (bi, 0, jnp.minimum((i + 1) * rr, nb - 1)))
    return pl.pallas_call(
        functools.partial(_attn_kernel, tq=tq),
        grid=grid,
        in_specs=[pl.BlockSpec(memory_space=pltpu.SMEM),
                  chan(1024), kprev, tok(K_COLS), knext, vprev, chan(512), vnext, tok(d),
                  _vmem_spec(),
                  pl.BlockSpec((1, d), lambda bi, i: (0, 0))],
        out_specs=tok(d),
        out_shape=jax.ShapeDtypeStruct((b, l, d), F32),
        scratch_shapes=[pltpu.VMEM((tq + 2 * BLK, K_COLS), BF16),
                        pltpu.VMEM((512, tq + 2 * BLK), BF16),
                        pltpu.VMEM((1024, tq), BF16)],
        compiler_params=_cparams(("parallel", "parallel")),
        name="attn_block",
    )(sink, q_t, k, k, k, v_t, v_t, v_t, x, wo, pw)


FF_CHUNK = 1024


def _ffn_kernel(x_ref, nw_ref, wu_ref, wd_ref, pw_ref, out_ref):
    x = x_ref[...]
    h = _rms(x, nw_ref[...], NORM_EPS).astype(BF16)
    acc = None
    for c in range(D_FF // FF_CHUNK):
        u = _dot(h, wu_ref[:, c * FF_CHUNK:(c + 1) * FF_CHUNK])
        a = jnp.square(jnp.maximum(u, 0.0)).astype(BF16)
        part = _dot(a, wd_ref[c * FF_CHUNK:(c + 1) * FF_CHUNK, :])
        acc = part if acc is None else acc + part
    out_ref[...] = x + _rms(acc, pw_ref[...], NORM_EPS)


def _ffn_call(x2d, nw, wu, wd, pw, tm):
    t, d = x2d.shape
    vec = pl.BlockSpec((1, d), lambda i: (0, 0))
    return pl.pallas_call(
        _ffn_kernel,
        grid=(t // tm,),
        in_specs=[pl.BlockSpec((tm, d), lambda i: (i, 0)), vec, _vmem_spec(), _vmem_spec(), vec],
        out_specs=pl.BlockSpec((tm, d), lambda i: (i, 0)),
        out_shape=jax.ShapeDtypeStruct((t, d), F32),
        compiler_params=_cparams(("parallel",)),
        name="ffn",
    )(x2d, nw, wu, wd, pw)


DT_COLS = 128
IN_EXT = D_INNER + CONV_DIM + DT_COLS


CONV_HALO = 8
CONV_STRIDE = 4
CONV_COLS = 512


def _inproj_kernel(xo_ref, xp_ref, xn_ref, nw_ref, w_ref, cw_ref, cb_ref, z_ref, xc_ref, dt_ref,
                   cbuf, sbuf, *, tm):
    i = pl.program_id(1)
    nsteps = pl.num_programs(1)
    x_ext = jnp.concatenate([xo_ref[...], xp_ref[...], xn_ref[...]], axis=0)
    h_ext = _rms(x_ext, nw_ref[...], NORM_EPS).astype(BF16)
    h = h_ext[0:tm]

    def gate_proj(c):
        z_ref[:, c * 512:(c + 1) * 512] = _dot(h, w_ref[:, c * 512:(c + 1) * 512]).astype(BF16)

    keep_prev = jnp.where(i > 0, 1.0, 0.0)
    keep_next = jnp.where(i < nsteps - 1, 1.0, 0.0)
    slabs = CONV_COLS // 128

    def project(cb):
        lo = D_INNER + cb * CONV_COLS
        y = _dot(h_ext, w_ref[:, lo:lo + CONV_COLS])
        for s in range(slabs):
            cs = cb * slabs + s
            cols = slice(s * 128, (s + 1) * 128)
            cbuf[cs, 0:CONV_HALO, :] = y[tm:tm + CONV_HALO, cols] * keep_prev
            cbuf[cs, CONV_HALO:CONV_HALO + tm, :] = y[0:tm, cols]
            cbuf[cs, CONV_HALO + tm:, :] = y[tm + CONV_HALO:, cols] * keep_next

    def conv(cb):
        for s in range(slabs):
            cs = cb * slabs + s
            cols = slice(cs * 128, (cs + 1) * 128)
            wk = [jnp.broadcast_to(cw_ref[k:k + 1, cols], (8, 128)) for k in range(D_CONV)]
            bk = jnp.broadcast_to(cb_ref[:, cols], (8, 128))
            for base in range(0, tm, 8 * CONV_STRIDE):
                for rr in range(CONV_STRIDE):
                    row = base + rr
                    acc = bk
                    for k in range(D_CONV):
                        acc = acc + wk[k] * cbuf[cs, pl.ds(CONV_HALO - CONV_PAD + k + row, 8, stride=CONV_STRIDE), :]
                    sbuf[cs, pl.ds(row, 8, stride=CONV_STRIDE), :] = acc * (1.0 / (1.0 + jnp.exp(-acc)))
            xc_ref[:, cols] = sbuf[cs].astype(BF16)

    nblocks = CONV_DIM // CONV_COLS
    project(0)
    for cb in range(nblocks):
        if cb + 1 < nblocks:
            project(cb + 1)
        conv(cb)
        if cb < D_INNER // 512:
            gate_proj(cb)
    dt_ref[...] = _dot(h, w_ref[:, D_INNER + CONV_DIM:])


def _inproj_call(x, nw, w_ext, cw, cb, tm):
    b, l, d = x.shape
    nh = l // CONV_HALO
    hpt = tm // CONV_HALO
    tok = lambda n: pl.BlockSpec((None, tm, n), lambda bi, i: (bi, i, 0))
    prev = pl.BlockSpec((None, CONV_HALO, d), lambda bi, i: (bi, jnp.maximum(i * hpt - 1, 0), 0))
    nxt = pl.BlockSpec((None, CONV_HALO, d), lambda bi, i: (bi, jnp.minimum((i + 1) * hpt, nh - 1), 0))
    full = lambda a: pl.BlockSpec(a.shape, lambda bi, i: (0,) * a.ndim)
    return pl.pallas_call(
        functools.partial(_inproj_kernel, tm=tm),
        grid=(b, l // tm),
        in_specs=[tok(d), prev, nxt, pl.BlockSpec((1, d), lambda bi, i: (0, 0)), _vmem_spec(), full(cw), full(cb)],
        out_specs=[tok(D_INNER), tok(CONV_DIM), tok(DT_COLS)],
        out_shape=[jax.ShapeDtypeStruct((b, l, D_INNER), BF16),
                   jax.ShapeDtypeStruct((b, l, CONV_DIM), BF16),
                   jax.ShapeDtypeStruct((b, l, DT_COLS), F32)],
        scratch_shapes=[pltpu.VMEM((CONV_DIM // 128, tm + 2 * CONV_HALO, 128), F32),
                        pltpu.VMEM((CONV_DIM // 128, tm, 128), F32)],
        compiler_params=_cparams(("parallel", "parallel")),
        name="in_proj",
    )(x, x, x, nw, w_ext, cw, cb)


def _softplus(x):
    return jnp.maximum(x, 0.0) + jnp.log1p(jnp.exp(-jnp.abs(x)))


def _tri_cumsum(tri, v):
    hi = v.astype(BF16)
    r1 = v - hi.astype(F32)
    mid = r1.astype(BF16)
    lo = (r1 - mid.astype(F32)).astype(BF16)
    return _dot(tri, hi) + _dot(tri, mid) + _dot(tri, lo)


def _expand_parts(vs, kinds):
    lane = lax.broadcasted_iota(jnp.int32, (BLK, DT_COLS), 1)
    first = lane < N_SSM_HEADS
    parts = []
    for v, kind in zip(vs, kinds):
        hi = v.astype(BF16).astype(F32)
        lo = v - hi
        if kind == "f":
            comb = jnp.where(first, hi, pltpu.roll(lo, N_SSM_HEADS, 1))
        else:
            comb = jnp.where(first, pltpu.roll(lo, DT_COLS - N_SSM_HEADS, 1), hi)
        parts.append(comb.astype(BF16))
    return jnp.concatenate(parts, axis=0)


def _dt_and_decay(dtraw_ref, rows, bias_ref, alog_ref):
    dt = _softplus(dtraw_ref[rows, :] + bias_ref[...])
    lane = lax.broadcasted_iota(jnp.int32, (1, DT_COLS), 1)
    a = jnp.where(lane < 2 * N_SSM_HEADS, -jnp.exp(alog_ref[...]), 0.0)
    return dt, dt * a


def _tri(lower):
    r = lax.broadcasted_iota(jnp.int32, (BLK, BLK), 0)
    c = lax.broadcasted_iota(jnp.int32, (BLK, BLK), 1)
    m = (c <= r) if lower else (c >= r)
    return jnp.where(m, 1.0, 0.0).astype(BF16)


SSD_TILE = 512
SSD_PIPE_DEPTH = 2


def _ssd_bwd_kernel(xc_ref, dtraw_ref, bias_ref, alog_ref, ee_ref, yoff_ref, hstate, *, nch):
    i = pl.program_id(1)

    @pl.when(i == 0)
    def _():
        hstate[...] = jnp.zeros_like(hstate)

    def chunk(it, carry):
        rows = pl.ds(pl.multiple_of((nch - 1 - it) * BLK, BLK), BLK)
        dt, da = _dt_and_decay(dtraw_ref, rows, bias_ref, alog_ref)
        rb = _tri_cumsum(_tri(False), da)
        tot = rb[0:1, :]
        ex = _dot(_expand_parts([jnp.exp(rb), dt * jnp.exp(tot - rb)], "bb"), ee_ref[...])
        for g in range(N_SSM_GROUPS):
            gc = slice(g * GROUP_COLS, (g + 1) * GROUP_COLS)
            bm = xc_ref[rows, D_INNER + g * D_STATE:D_INNER + (g + 1) * D_STATE]
            cm = xc_ref[rows, D_INNER + 512 + g * D_STATE:D_INNER + 512 + (g + 1) * D_STATE]
            hg = hstate[g]
            yoff_ref[rows, gc] = (_dot(cm, hg.astype(BF16)) * ex[0:BLK, gc]).astype(BF16)
            xs = xc_ref[rows, gc] * ex[BLK:2 * BLK, gc].astype(BF16)
            hstate[g] = hg * ex[0:1, gc] + _dot_tn(bm, xs)
        return carry

    lax.fori_loop(0, nch, chunk, 0)


def _ssd_bwd_call(xc, dtraw, bias, alog, ee):
    b, l, _ = xc.shape
    ts = _pick(l, SSD_TILE)
    nt = l // ts
    own = lambda n: pl.BlockSpec((None, ts, n), lambda bi, i: (bi, nt - 1 - i, 0))
    full = lambda a: pl.BlockSpec(a.shape, lambda bi, i: (0,) * a.ndim)
    return pl.pallas_call(
        functools.partial(_ssd_bwd_kernel, nch=ts // BLK),
        grid=(b, nt),
        in_specs=[own(CONV_DIM), own(DT_COLS), full(bias), full(alog), _vmem_spec()],
        out_specs=own(D_INNER),
        out_shape=jax.ShapeDtypeStruct((b, l, D_INNER), BF16),
        scratch_shapes=[pltpu.VMEM((N_SSM_GROUPS, D_STATE, GROUP_COLS), F32)],
        compiler_params=_cparams(("parallel", "arbitrary")),
        name="ssd_bwd",
    )(xc, dtraw, bias, alog, ee)


def _ssd_fwd_kernel(xc_ref, dtraw_ref, z_ref, yoffb_ref, xres_ref, bias_ref, alog_ref, dskip_ref,
                    gnw_ref, wout_ref, pw_ref, ee_ref, out_ref, hstate, ybuf, *, nch):
    i = pl.program_id(1)

    @pl.when(i == 0)
    def _():
        hstate[...] = jnp.zeros_like(hstate)

    def chunk(ci):
        rows = slice(ci * BLK, (ci + 1) * BLK)
        dt, da = _dt_and_decay(dtraw_ref, rows, bias_ref, alog_ref)
        cf = _tri_cumsum(_tri(True), da)
        rb = _tri_cumsum(_tri(False), da)
        tot = cf[BLK - 1:BLK, :]
        cf2 = cf * LOG2E
        rb2 = rb * LOG2E
        r = lax.broadcasted_iota(jnp.int32, (BLK, BLK), 0)
        c = lax.broadcasted_iota(jnp.int32, (BLK, BLK), 1)
        src_t = jnp.where(r < N_SSM_HEADS, cf2.T, rb2.T) - jnp.log2(dt.T)
        lower = c <= r
        keep_lo = jnp.where(c < HEAD_DIM, 1.0, 0.0).astype(BF16)
        keep_hi = jnp.where(c < HEAD_DIM, 0.0, 1.0).astype(BF16)

        cbd = jnp.zeros((BLK, DT_COLS), F32)
        for g in range(N_SSM_GROUPS):
            bm = xc_ref[rows, D_INNER + g * D_STATE:D_INNER + (g + 1) * D_STATE].astype(F32)
            cm = xc_ref[rows, D_INNER + 512 + g * D_STATE:D_INNER + 512 + (g + 1) * D_STATE].astype(F32)
            d = jnp.sum(bm * cm, axis=1, keepdims=True)
            sel = (c >= N_SSM_HEADS + 8 * g) & (c < N_SSM_HEADS + 8 * (g + 1))
            cbd = jnp.where(sel, d, cbd)
        parts = _expand_parts([jnp.exp(cf), dt * jnp.exp(tot - cf), cbd * dt + dskip_ref[...]], "ffb")

        def pair_out(pc, ms, yr):
            xp = xc_ref[rows, pc]
            rhs = jnp.concatenate([xp * keep_lo, xp * keep_hi], axis=0)
            ybuf[rows, pc] = _dot(jnp.concatenate(ms, axis=1), rhs) + yr

        pending = []
        for g in range(N_SSM_GROUPS):
            gc = slice(g * GROUP_COLS, (g + 1) * GROUP_COLS)
            bm = xc_ref[rows, D_INNER + g * D_STATE:D_INNER + (g + 1) * D_STATE]
            cm = xc_ref[rows, D_INNER + 512 + g * D_STATE:D_INNER + 512 + (g + 1) * D_STATE]
            cbm = _dot_nt(cm, bm)
            ex = _dot(parts, ee_ref[:, gc])
            hg = hstate[g]
            xg = xc_ref[rows, gc]
            yrest = (_dot(cm, hg.astype(BF16)) * ex[0:BLK]
                     + yoffb_ref[rows, gc].astype(F32)
                     + xg.astype(F32) * ex[2 * BLK:3 * BLK])
            xs = xg * ex[BLK:2 * BLK].astype(BF16)
            hstate[g] = hg * ex[BLK - 1:BLK] + _dot_tn(bm, xs)
            for pr in range(4):
                ms = []
                for t in range(2):
                    h = g * 8 + pr * 2 + t
                    hb = N_SSM_HEADS + h
                    dmat = jnp.where(lower, cf2[:, h:h + 1] - src_t[h:h + 1, :],
                                     rb2[:, hb:hb + 1] - src_t[hb:hb + 1, :])
                    ms.append((cbm * jnp.exp2(dmat)).astype(BF16))
                pc = slice(g * GROUP_COLS + pr * 128, g * GROUP_COLS + (pr + 1) * 128)
                pending.append((pc, ms, yrest[:, pr * 128:(pr + 1) * 128]))
                if len(pending) > SSD_PIPE_DEPTH:
                    pair_out(*pending.pop(0))
        for item in pending:
            pair_out(*item)

    def finish(ci):
        rows = slice(ci * BLK, (ci + 1) * BLK)
        zf = z_ref[rows, :].astype(F32)
        y = ybuf[rows, :] * (zf * (1.0 / (1.0 + jnp.exp(-zf))))
        scale = lax.rsqrt(jnp.mean(y * y, axis=-1, keepdims=True) + GATED_NORM_EPS)
        o = _dot((y * gnw_ref[...]).astype(BF16), wout_ref[...]) * scale
        out_ref[rows, :] = xres_ref[rows, :] + _rms(o, pw_ref[...], NORM_EPS)

    for ci in range(nch):
        chunk(ci)
        if ci > 0:
            finish(ci - 1)
    finish(nch - 1)


def _ssd_fwd_call(xc, dtraw, z, yoffb, xres, bias, alog, dskip, gnw, wout, pw, ee):
    b, l, d = xres.shape
    ts = _pick(l, SSD_TILE)
    own = lambda n: pl.BlockSpec((None, ts, n), lambda bi, i: (bi, i, 0))
    full = lambda a: pl.BlockSpec(a.shape, lambda bi, i: (0,) * a.ndim)
    return pl.pallas_call(
        functools.partial(_ssd_fwd_kernel, nch=ts // BLK),
        grid=(b, l // ts),
        in_specs=[own(CONV_DIM), own(DT_COLS), own(D_INNER), own(D_INNER), own(d),
                  full(bias), full(alog), full(dskip), full(gnw), _vmem_spec(), full(pw), _vmem_spec()],
        out_specs=own(d),
        out_shape=jax.ShapeDtypeStruct((b, l, d), F32),
        scratch_shapes=[pltpu.VMEM((N_SSM_GROUPS, D_STATE, GROUP_COLS), F32),
                        pltpu.VMEM((ts, D_INNER), F32)],
        compiler_params=_cparams(("parallel", "arbitrary")),
        name="ssd_fwd",
    )(xc, dtraw, z, yoffb, xres, bias, alog, dskip, gnw, wout, pw, ee)


def _rope_tables(l):
    inv_freq = 1.0 / (ROPE_THETA ** (jnp.arange(0, HEAD_DIM, 2, dtype=F32) / HEAD_DIM))
    ang = jnp.arange(l, dtype=F32)[:, None] * inv_freq[None, :]
    cos = jnp.cos(ang)
    sin = jnp.sin(ang)
    cos_t = jnp.concatenate([cos, cos, cos, cos], axis=1)
    sin_t = jnp.concatenate([-sin, sin, -sin, sin], axis=1)
    return cos_t, sin_t, cos.T, sin.T


def _expand_matrix():
    rows = jnp.arange(DT_COLS)
    heads = jnp.arange(D_INNER) // HEAD_DIM
    hit = ((rows[:, None] % N_SSM_HEADS) == heads[None, :]) & (rows[:, None] < 2 * N_SSM_HEADS)
    return hit.astype(BF16)


def _prep(attn_w_qkv, attn_w_o, attn_sink, ssm_w_in, ssm_conv_w, ssm_conv_b, ssm_dt_bias,
          ssm_a_log, ssm_d, ssm_norm_w, ssm_w_out, norm_mix_pre, norm_mix_post,
          norm_ffn_pre, norm_ffn_post, mlp_w_up, mlp_w_down):
    p = {}
    wq = attn_w_qkv[0]
    nq = N_Q_HEADS * HEAD_DIM
    nk = N_KV_HEADS * HEAD_DIM
    p["w_qv_t"] = jnp.concatenate([wq[:, :nq], wq[:, nq + nk:]], axis=1).T.astype(BF16)
    p["w_k"] = wq[:, nq:nq + nk].astype(BF16)
    p["w_o"] = attn_w_o[0].astype(BF16)
    p["sink"] = attn_sink[0].astype(F32)
    w_in = ssm_w_in[0]
    p["w_in"] = jnp.concatenate([w_in, jnp.zeros((D_MODEL, DT_COLS - 2 * N_SSM_HEADS), F32)], axis=1).astype(BF16)
    p["conv_w"] = ssm_conv_w[0].astype(F32)
    p["conv_b"] = ssm_conv_b[0].astype(F32).reshape(1, CONV_DIM)
    pad = jnp.zeros((DT_COLS - 2 * N_SSM_HEADS,), F32)
    p["dt_bias"] = jnp.concatenate([ssm_dt_bias[0].astype(F32).reshape(-1), pad]).reshape(1, DT_COLS)
    p["a_log"] = jnp.concatenate([ssm_a_log[0].astype(F32).reshape(-1), pad]).reshape(1, DT_COLS)
    p["d_skip"] = jnp.concatenate([jnp.zeros((N_SSM_HEADS,), F32), ssm_d[0].astype(F32), pad]).reshape(1, DT_COLS)
    p["gnw"] = ssm_norm_w[0].astype(F32).reshape(1, D_INNER)
    p["w_out"] = ssm_w_out[0].astype(BF16)
    p["mix_pre"] = norm_mix_pre.astype(F32).reshape(-1, 1, D_MODEL)
    p["mix_post"] = norm_mix_post.astype(F32).reshape(-1, 1, D_MODEL)
    p["ffn_pre"] = norm_ffn_pre.astype(F32).reshape(-1, 1, D_MODEL)
    p["ffn_post"] = norm_ffn_post.astype(F32).reshape(-1, 1, D_MODEL)
    p["w_up"] = mlp_w_up.astype(BF16)
    p["w_down"] = mlp_w_down.astype(BF16)
    p["ee"] = _expand_matrix()
    return p


def _pick(n, pref):
    t = pref
    while n % t:
        t //= 2
    return t


def _trunk(x, p):
    b, l, d = x.shape
    t = b * l
    tm = _pick(t, 512)
    tq = _pick(l, 512)
    cos_t, sin_t, cos_tt, sin_tt = _rope_tables(l)
    q_t, k, v_t = _qkv_call(x, p["mix_pre"][0], p["w_qv_t"], p["w_k"], cos_t, sin_t, cos_tt, sin_tt,
                            _pick(l, 512))
    x = _attn_call(p["sink"], q_t, k, v_t, x, p["w_o"], p["mix_post"][0], tq)
    x = _ffn_call(x.reshape(t, d), p["ffn_pre"][0], p["w_up"][0], p["w_down"][0], p["ffn_post"][0], tm)
    x = x.reshape(b, l, d)
    z, xc, dtraw = _inproj_call(x, p["mix_pre"][1], p["w_in"], p["conv_w"], p["conv_b"], _pick(l, 512))
    yoffb = _ssd_bwd_call(xc, dtraw, p["dt_bias"], p["a_log"], p["ee"])
    x = _ssd_fwd_call(xc, dtraw, z, yoffb, x,
                      p["dt_bias"], p["a_log"], p["d_skip"], p["gnw"], p["w_out"], p["mix_post"][1],
                      p["ee"])
    x = _ffn_call(x.reshape(t, d), p["ffn_pre"][1], p["w_up"][1], p["w_down"][1], p["ffn_post"][1], tm)
    return x.reshape(b, l, d)


def kernel(x_prompt, x_sample, attn_w_qkv, attn_w_o, attn_sink, ssm_w_in, ssm_conv_w, ssm_conv_b,
           ssm_dt_bias, ssm_a_log, ssm_d, ssm_norm_w, ssm_w_out, norm_mix_pre, norm_mix_post,
           norm_ffn_pre, norm_ffn_post, mlp_w_up, mlp_w_down):
    p = _prep(attn_w_qkv, attn_w_o, attn_sink, ssm_w_in, ssm_conv_w, ssm_conv_b, ssm_dt_bias,
              ssm_a_log, ssm_d, ssm_norm_w, ssm_w_out, norm_mix_pre, norm_mix_post,
              norm_ffn_pre, norm_ffn_post, mlp_w_up, mlp_w_down)
    return (_trunk(x_prompt, p), _trunk(x_sample, p))
```

```python
import functools
import math

import jax
import jax.numpy as jnp
from jax import lax
from jax.experimental import pallas as pl
from jax.experimental.pallas import tpu as pltpu

F32 = jnp.float32
BF16 = jnp.bfloat16

D_MODEL = 1024
HEAD_DIM = 64
N_Q_HEADS = 16
N_KV_HEADS = 4
Q_PER_KV = 4
WINDOW = 128
BLK = 128
ROPE_THETA = 10000.0
D_INNER = 2048
N_SSM_HEADS = 32
N_SSM_GROUPS = 4
D_STATE = 128
D_CONV = 5
CONV_PAD = 2
CONV_DIM = D_INNER + 2 * N_SSM_GROUPS * D_STATE
GROUP_COLS = D_INNER // N_SSM_GROUPS
D_FF = 4096
NORM_EPS = 1e-6
GATED_NORM_EPS = 1e-5
LOG2E = 1.4426950408889634
NEG_BIG = -1e30

VMEM_LIMIT = 56 * 1024 * 1024


def _cparams(sem):
    return pltpu.CompilerParams(dimension_semantics=sem, vmem_limit_bytes=VMEM_LIMIT)


def _vmem_spec():
    return pl.BlockSpec(memory_space=pltpu.VMEM)


def _rms(x, w, eps):
    ms = jnp.mean(x * x, axis=-1, keepdims=True)
    return x * lax.rsqrt(ms + eps) * w


def _dot(a, b):
    return jnp.dot(a, b, preferred_element_type=F32)


def _dot_nt(a, b):
    return lax.dot_general(a, b, (((1,), (1,)), ((), ())), preferred_element_type=F32)


def _dot_tn(a, b):
    return lax.dot_general(a, b, (((0,), (0,)), ((), ())), preferred_element_type=F32)


K_COLS = N_KV_HEADS * HEAD_DIM
HALF = HEAD_DIM // 2


def _qkv_kernel(x_ref, nw_ref, wqv_ref, wk_ref, cos_ref, sin_ref, cost_ref, sint_ref, q_ref, k_ref, v_ref):
    h = _rms(x_ref[...], nw_ref[...], NORM_EPS).astype(BF16)
    yk = _dot(h, wk_ref[...])
    cos = cos_ref[...]
    sin = sin_ref[...]
    lane = lax.broadcasted_iota(jnp.int32, cos.shape, 1)
    first_half = (lane & HALF) == 0
    for s in range(K_COLS // 128):
        yc = yk[:, s * 128:(s + 1) * 128]
        rot = jnp.where(first_half, pltpu.roll(yc, 128 - HALF, 1), pltpu.roll(yc, HALF, 1))
        k_ref[:, s * 128:(s + 1) * 128] = (yc * cos + rot * sin).astype(BF16)
    cost = cost_ref[...]
    sint = sint_ref[...]
    qscale = HEAD_DIM ** -0.5 * LOG2E
    yqv = _dot_nt(wqv_ref[...], h)
    for hd in range(N_Q_HEADS):
        t1 = yqv[hd * HEAD_DIM:hd * HEAD_DIM + HALF]
        t2 = yqv[hd * HEAD_DIM + HALF:(hd + 1) * HEAD_DIM]
        q_ref[hd * HEAD_DIM:hd * HEAD_DIM + HALF, :] = ((t1 * cost - t2 * sint) * qscale).astype(BF16)
        q_ref[hd * HEAD_DIM + HALF:(hd + 1) * HEAD_DIM, :] = ((t2 * cost + t1 * sint) * qscale).astype(BF16)
    ones = jnp.ones((HEAD_DIM, h.shape[0]), BF16)
    for g in range(N_KV_HEADS):
        lo = N_Q_HEADS * HEAD_DIM + g * HEAD_DIM
        v_ref[g * 128:g * 128 + HEAD_DIM, :] = yqv[lo:lo + HEAD_DIM].astype(BF16)
        v_ref[g * 128 + HEAD_DIM:(g + 1) * 128, :] = ones


def _qkv_call(x, nw, wqv_t, wk, cos_t, sin_t, cos_tt, sin_tt, tm):
    b, l, d = x.shape
    grid = (b, l // tm)
    tok = lambda n: pl.BlockSpec((None, tm, n), lambda bi, i: (bi, i, 0))
    chan = lambda n: pl.BlockSpec((None, n, tm), lambda bi, i: (bi, 0, i))
    return pl.pallas_call(
        _qkv_kernel,
        grid=grid,
        in_specs=[tok(d),
                  pl.BlockSpec((1, d), lambda bi, i: (0, 0)),
                  _vmem_spec(), _vmem_spec(),
                  pl.BlockSpec((tm, 128), lambda bi, i: (i, 0)),
                  pl.BlockSpec((tm, 128), lambda bi, i: (i, 0)),
                  pl.BlockSpec((HALF, tm), lambda bi, i: (0, i)),
                  pl.BlockSpec((HALF, tm), lambda bi, i: (0, i))],
        out_specs=[chan(1024), tok(K_COLS), chan(512)],
        out_shape=[jax.ShapeDtypeStruct((b, 1024, l), BF16),
                   jax.ShapeDtypeStruct((b, l, K_COLS), BF16),
                   jax.ShapeDtypeStruct((b, 512, l), BF16)],
        compiler_params=_cparams(("parallel", "parallel")),
        name="qkv_rope",
    )(x, nw, wqv_t, wk, cos_t, sin_t, cos_tt, sin_tt)


ATTN_PIPE_DEPTH = 2
ATTN_TILE = 1024


def _attn_kernel(sink_ref, q_ref, kp_ref, ko_ref, kn_ref, vp_ref, vo_ref, vn_ref, x_ref,
                 wo_ref, pw_ref, out_ref, kbuf, vbuf, obuf, *, tq):
    i = pl.program_id(1)
    nsteps = pl.num_programs(1)
    nsub = tq // BLK
    kbuf[0:BLK, :] = kp_ref[...]
    kbuf[BLK:BLK + tq, :] = ko_ref[...]
    kbuf[BLK + tq:, :] = kn_ref[...]
    vbuf[:, 0:BLK] = vp_ref[...]
    vbuf[:, BLK:BLK + tq] = vo_ref[...]
    vbuf[:, BLK + tq:] = vn_ref[...]

    c = lax.broadcasted_iota(jnp.int32, (3 * BLK, BLK), 0)
    r = lax.broadcasted_iota(jnp.int32, (3 * BLK, BLK), 1)
    er = lax.broadcasted_iota(jnp.int32, (BLK, BLK), 0)
    ec = lax.broadcasted_iota(jnp.int32, (BLK, BLK), 1)
    eye = jnp.where(er == ec, 1.0, 0.0).astype(BF16)
    eye4 = jnp.concatenate([eye] * Q_PER_KV, axis=1)
    zero_half = jnp.zeros((HEAD_DIM, BLK), BF16)
    first_lo = jnp.where(i > 0, 0, BLK)
    last_hi = jnp.where(i < nsteps - 1, 3 * BLK, 2 * BLK)

    def scores(j, g, bias):
        qcols = slice(j * BLK, (j + 1) * BLK)
        kk = kbuf[j * BLK:(j + 3) * BLK, (g // 2) * 128:(g // 2 + 1) * 128]
        blocks = []
        for t in range(Q_PER_KV):
            hd = g * Q_PER_KV + t
            qh = q_ref[hd * HEAD_DIM:(hd + 1) * HEAD_DIM, qcols]
            blocks.append(jnp.concatenate([qh, zero_half] if g % 2 == 0 else [zero_half, qh], axis=0))
        rhs = jnp.concatenate([jnp.concatenate(blocks, axis=1), eye4], axis=0)
        s = _dot(jnp.concatenate([kk, bias], axis=1), rhs)
        sink = jnp.concatenate(
            [jnp.full((1, BLK), sink_ref[g * Q_PER_KV + t] * LOG2E, F32) for t in range(Q_PER_KV)], axis=1)
        m = jnp.maximum(jnp.max(s, axis=0, keepdims=True), sink)
        return s, m, sink

    def finish(j, g, s, m, sink):
        qcols = slice(j * BLK, (j + 1) * BLK)
        p = jnp.exp2(s - m).astype(BF16)
        o = _dot(vbuf[g * 128:(g + 1) * 128, j * BLK:(j + 3) * BLK], p)
        den = o[HEAD_DIM:] + jnp.exp2(sink - m)
        on = (o[:HEAD_DIM] * (1.0 / den)).astype(BF16)
        for t in range(Q_PER_KV):
            hd = g * Q_PER_KV + t
            obuf[hd * HEAD_DIM:(hd + 1) * HEAD_DIM, qcols] = on[:, t * BLK:(t + 1) * BLK]

    pending = []
    for j in range(nsub):
        cmin = r
        cmax = r + 2 * WINDOW
        if j == 0:
            cmin = jnp.maximum(cmin, first_lo)
        if j == nsub - 1:
            cmax = jnp.minimum(cmax, last_hi - 1)
        bias = jnp.where(c >= cmin, jnp.where(c <= cmax, 0.0, NEG_BIG), NEG_BIG).astype(BF16)
        for g in range(N_KV_HEADS):
            pending.append((j, g) + scores(j, g, bias))
            if len(pending) > ATTN_PIPE_DEPTH:
                finish(*pending.pop(0))
    for item in pending:
        finish(*item)
    a = _dot_tn(obuf[...], wo_ref[...])
    out_ref[...] = x_ref[...] + _rms(a, pw_ref[...], NORM_EPS)


def _attn_call(sink, q_t, k, v_t, x, wo, pw, tq):
    b, l, d = x.shape
    nb = l // BLK
    rr = tq // BLK
    grid = (b, l // tq)
    tok = lambda n: pl.BlockSpec((None, tq, n), lambda bi, i: (bi, i, 0))
    chan = lambda n: pl.BlockSpec((None, n, tq), lambda bi, i: (bi, 0, i))
    kprev = pl.BlockSpec((None, BLK, K_COLS), lambda bi, i: (bi, jnp.maximum(i * rr - 1, 0), 0))
    knext = pl.BlockSpec((None, BLK, K_COLS), lambda bi, i: (bi, jnp.minimum((i + 1) * rr, nb - 1), 0))
    vprev = pl.BlockSpec((None, 512, BLK), lambda bi, i: (bi, 0, jnp.maximum(i * rr - 1, 0)))
    vnext = pl.BlockSpec((None, 512, BLK), lambda bi, i: (bi, 0, jnp.minimum((i + 1) * rr, nb - 1)))
    return pl.pallas_call(
        functools.partial(_attn_kernel, tq=tq),
        grid=grid,
        in_specs=[pl.BlockSpec(memory_space=pltpu.SMEM),
                  chan(1024), kprev, tok(K_COLS), knext, vprev, chan(512), vnext, tok(d),
                  _vmem_spec(),
                  pl.BlockSpec((1, d), lambda bi, i: (0, 0))],
        out_specs=tok(d),
        out_shape=jax.ShapeDtypeStruct((b, l, d), F32),
        scratch_shapes=[pltpu.VMEM((tq + 2 * BLK, K_COLS), BF16),
                        pltpu.VMEM((512, tq + 2 * BLK), BF16),
                        pltpu.VMEM((1024, tq), BF16)],
        compiler_params=_cparams(("parallel", "parallel")),
        name="attn_block",
    )(sink, q_t, k, k, k, v_t, v_t, v_t, x, wo, pw)


FF_CHUNK = 1024
FFN_TILE = 1024


def _ffn_kernel(x_ref, nw_ref, wu_ref, wd_ref, pw_ref, out_ref):
    x = x_ref[...]
    h = _rms(x, nw_ref[...], NORM_EPS).astype(BF16)
    acc = None
    for c in range(D_FF // FF_CHUNK):
        u = _dot(h, wu_ref[:, c * FF_CHUNK:(c + 1) * FF_CHUNK])
        a = jnp.square(jnp.maximum(u, 0.0)).astype(BF16)
        part = _dot(a, wd_ref[c * FF_CHUNK:(c + 1) * FF_CHUNK, :])
        acc = part if acc is None else acc + part
    out_ref[...] = x + _rms(acc, pw_ref[...], NORM_EPS)


def _ffn_call(x2d, nw, wu, wd, pw, tm):
    t, d = x2d.shape
    vec = pl.BlockSpec((1, d), lambda i: (0, 0))
    return pl.pallas_call(
        _ffn_kernel,
        grid=(t // tm,),
        in_specs=[pl.BlockSpec((tm, d), lambda i: (i, 0)), vec, _vmem_spec(), _vmem_spec(), vec],
        out_specs=pl.BlockSpec((tm, d), lambda i: (i, 0)),
        out_shape=jax.ShapeDtypeStruct((t, d), F32),
        compiler_params=_cparams(("parallel",)),
        name="ffn",
    )(x2d, nw, wu, wd, pw)


DT_COLS = 128
IN_EXT = D_INNER + CONV_DIM + DT_COLS


CONV_HALO = 8
CONV_STRIDE = 4
CONV_COLS = 512


def _inproj_kernel(xo_ref, xp_ref, xn_ref, nw_ref, w_ref, cw_ref, cb_ref, z_ref, xc_ref, dt_ref,
                   cbuf, sbuf, *, tm):
    i = pl.program_id(1)
    nsteps = pl.num_programs(1)
    x_ext = jnp.concatenate([xo_ref[...], xp_ref[...], xn_ref[...]], axis=0)
    h_ext = _rms(x_ext, nw_ref[...], NORM_EPS).astype(BF16)
    h = h_ext[0:tm]

    def gate_proj(c):
        z_ref[:, c * 512:(c + 1) * 512] = _dot(h, w_ref[:, c * 512:(c + 1) * 512]).astype(BF16)

    keep_prev = jnp.where(i > 0, 1.0, 0.0)
    keep_next = jnp.where(i < nsteps - 1, 1.0, 0.0)
    slabs = CONV_COLS // 128

    def project(cb):
        lo = D_INNER + cb * CONV_COLS
        y = _dot(h_ext, w_ref[:, lo:lo + CONV_COLS])
        for s in range(slabs):
            cs = cb * slabs + s
            cols = slice(s * 128, (s + 1) * 128)
            cbuf[cs, 0:CONV_HALO, :] = y[tm:tm + CONV_HALO, cols] * keep_prev
            cbuf[cs, CONV_HALO:CONV_HALO + tm, :] = y[0:tm, cols]
            cbuf[cs, CONV_HALO + tm:, :] = y[tm + CONV_HALO:, cols] * keep_next

    def conv(cb):
        for s in range(slabs):
            cs = cb * slabs + s
            cols = slice(cs * 128, (cs + 1) * 128)
            wk = [jnp.broadcast_to(cw_ref[k:k + 1, cols], (8, 128)) for k in range(D_CONV)]
            bk = jnp.broadcast_to(cb_ref[:, cols], (8, 128))
            for base in range(0, tm, 8 * CONV_STRIDE):
                for rr in range(CONV_STRIDE):
                    row = base + rr
                    acc = bk
                    for k in range(D_CONV):
                        acc = acc + wk[k] * cbuf[cs, pl.ds(CONV_HALO - CONV_PAD + k + row, 8, stride=CONV_STRIDE), :]
                    sbuf[cs, pl.ds(row, 8, stride=CONV_STRIDE), :] = acc * (1.0 / (1.0 + jnp.exp(-acc)))
            xc_ref[:, cols] = sbuf[cs].astype(BF16)

    nblocks = CONV_DIM // CONV_COLS
    project(0)
    for cb in range(nblocks):
        if cb + 1 < nblocks:
            project(cb + 1)
        conv(cb)
        if cb < D_INNER // 512:
            gate_proj(cb)
    dt_ref[...] = _dot(h, w_ref[:, D_INNER + CONV_DIM:])


def _inproj_call(x, nw, w_ext, cw, cb, tm):
    b, l, d = x.shape
    nh = l // CONV_HALO
    hpt = tm // CONV_HALO
    tok = lambda n: pl.BlockSpec((None, tm, n), lambda bi, i: (bi, i, 0))
    prev = pl.BlockSpec((None, CONV_HALO, d), lambda bi, i: (bi, jnp.maximum(i * hpt - 1, 0), 0))
    nxt = pl.BlockSpec((None, CONV_HALO, d), lambda bi, i: (bi, jnp.minimum((i + 1) * hpt, nh - 1), 0))
    full = lambda a: pl.BlockSpec(a.shape, lambda bi, i: (0,) * a.ndim)
    return pl.pallas_call(
        functools.partial(_inproj_kernel, tm=tm),
        grid=(b, l // tm),
        in_specs=[tok(d), prev, nxt, pl.BlockSpec((1, d), lambda bi, i: (0, 0)), _vmem_spec(), full(cw), full(cb)],
        out_specs=[tok(D_INNER), tok(CONV_DIM), tok(DT_COLS)],
        out_shape=[jax.ShapeDtypeStruct((b, l, D_INNER), BF16),
                   jax.ShapeDtypeStruct((b, l, CONV_DIM), BF16),
                   jax.ShapeDtypeStruct((b, l, DT_COLS), F32)],
        scratch_shapes=[pltpu.VMEM((CONV_DIM // 128, tm + 2 * CONV_HALO, 128), F32),
                        pltpu.VMEM((CONV_DIM // 128, tm, 128), F32)],
        compiler_params=_cparams(("parallel", "parallel")),
        name="in_proj",
    )(x, x, x, nw, w_ext, cw, cb)


def _softplus(x):
    return jnp.maximum(x, 0.0) + jnp.log1p(jnp.exp(-jnp.abs(x)))


def _pieces(v):
    hi = v.astype(BF16)
    r1 = v - hi.astype(F32)
    mid = r1.astype(BF16)
    lo = (r1 - mid.astype(F32)).astype(BF16)
    return hi, mid, lo


def _tri_cumsum(tri, v):
    hi, mid, lo = _pieces(v)
    return _dot(tri, hi) + _dot(tri, mid) + _dot(tri, lo)


def _tri_cumsum_stacked(tri, v):
    st = _dot(tri, jnp.concatenate(_pieces(v), axis=1))
    return st[:, 0:BLK] + st[:, BLK:2 * BLK] + st[:, 2 * BLK:3 * BLK]


def _expand_parts(vs, kinds):
    lane = lax.broadcasted_iota(jnp.int32, (BLK, DT_COLS), 1)
    first = lane < N_SSM_HEADS
    parts = []
    for v, kind in zip(vs, kinds):
        hi = v.astype(BF16).astype(F32)
        lo = v - hi
        if kind == "f":
            comb = jnp.where(first, hi, pltpu.roll(lo, N_SSM_HEADS, 1))
        else:
            comb = jnp.where(first, pltpu.roll(lo, DT_COLS - N_SSM_HEADS, 1), hi)
        parts.append(comb.astype(BF16))
    return jnp.concatenate(parts, axis=0)


def _dt_and_decay(dtraw_ref, rows, bias_ref, alog_ref):
    dt = _softplus(dtraw_ref[rows, :] + bias_ref[...])
    lane = lax.broadcasted_iota(jnp.int32, (1, DT_COLS), 1)
    a = jnp.where(lane < 2 * N_SSM_HEADS, -jnp.exp(alog_ref[...]), 0.0)
    return dt, dt * a


def _tri(lower):
    r = lax.broadcasted_iota(jnp.int32, (BLK, BLK), 0)
    c = lax.broadcasted_iota(jnp.int32, (BLK, BLK), 1)
    m = (c <= r) if lower else (c >= r)
    return jnp.where(m, 1.0, 0.0).astype(BF16)


SSD_TILE = 512
SSD_PIPE_DEPTH = 2


def _ssd_bwd_kernel(xc_ref, dtraw_ref, bias_ref, alog_ref, ee_ref, yoff_ref, hstate, *, nch):
    i = pl.program_id(1)

    @pl.when(i == 0)
    def _():
        hstate[...] = jnp.zeros_like(hstate)

    def chunk(it, carry):
        rows = pl.ds(pl.multiple_of((nch - 1 - it) * BLK, BLK), BLK)
        dt, da = _dt_and_decay(dtraw_ref, rows, bias_ref, alog_ref)
        rb = _tri_cumsum_stacked(_tri(False), da)
        tot = rb[0:1, :]
        ex = _dot(_expand_parts([jnp.exp(rb), dt * jnp.exp(tot - rb)], "bb"), ee_ref[...])
        for g in range(N_SSM_GROUPS):
            gc = slice(g * GROUP_COLS, (g + 1) * GROUP_COLS)
            bm = xc_ref[rows, D_INNER + g * D_STATE:D_INNER + (g + 1) * D_STATE]
            cm = xc_ref[rows, D_INNER + 512 + g * D_STATE:D_INNER + 512 + (g + 1) * D_STATE]
            hg = hstate[g]
            yoff_ref[rows, gc] = (_dot(cm, hg.astype(BF16)) * ex[0:BLK, gc]).astype(BF16)
            xs = xc_ref[rows, gc] * ex[BLK:2 * BLK, gc].astype(BF16)
            hstate[g] = hg * ex[0:1, gc] + _dot_tn(bm, xs)
        return carry

    lax.fori_loop(0, nch, chunk, 0)


def _ssd_bwd_call(xc, dtraw, bias, alog, ee):
    b, l, _ = xc.shape
    ts = _pick(l, SSD_TILE)
    nt = l // ts
    own = lambda n: pl.BlockSpec((None, ts, n), lambda bi, i: (bi, nt - 1 - i, 0))
    full = lambda a: pl.BlockSpec(a.shape, lambda bi, i: (0,) * a.ndim)
    return pl.pallas_call(
        functools.partial(_ssd_bwd_kernel, nch=ts // BLK),
        grid=(b, nt),
        in_specs=[own(CONV_DIM), own(DT_COLS), full(bias), full(alog), _vmem_spec()],
        out_specs=own(D_INNER),
        out_shape=jax.ShapeDtypeStruct((b, l, D_INNER), BF16),
        scratch_shapes=[pltpu.VMEM((N_SSM_GROUPS, D_STATE, GROUP_COLS), F32)],
        compiler_params=_cparams(("parallel", "arbitrary")),
        name="ssd_bwd",
    )(xc, dtraw, bias, alog, ee)


def _ssd_fwd_kernel(xc_ref, dtraw_ref, z_ref, yoffb_ref, xres_ref, bias_ref, alog_ref, dskip_ref,
                    gnw_ref, wout_ref, pw_ref, ee_ref, out_ref, hstate, ybuf, *, nch):
    i = pl.program_id(1)

    @pl.when(i == 0)
    def _():
        hstate[...] = jnp.zeros_like(hstate)

    def chunk(ci):
        rows = slice(ci * BLK, (ci + 1) * BLK)
        dt, da = _dt_and_decay(dtraw_ref, rows, bias_ref, alog_ref)
        cf = _tri_cumsum(_tri(True), da)
        rb = _tri_cumsum(_tri(False), da)
        tot = cf[BLK - 1:BLK, :]
        cf2 = cf * LOG2E
        rb2 = rb * LOG2E
        r = lax.broadcasted_iota(jnp.int32, (BLK, BLK), 0)
        c = lax.broadcasted_iota(jnp.int32, (BLK, BLK), 1)
        src_t = jnp.where(r < N_SSM_HEADS, cf2.T, rb2.T) - jnp.log2(dt.T)
        lower = c <= r
        keep_lo = jnp.where(c < HEAD_DIM, 1.0, 0.0).astype(BF16)
        keep_hi = jnp.where(c < HEAD_DIM, 0.0, 1.0).astype(BF16)

        cbd = jnp.zeros((BLK, DT_COLS), F32)
        for g in range(N_SSM_GROUPS):
            bm = xc_ref[rows, D_INNER + g * D_STATE:D_INNER + (g + 1) * D_STATE].astype(F32)
            cm = xc_ref[rows, D_INNER + 512 + g * D_STATE:D_INNER + 512 + (g + 1) * D_STATE].astype(F32)
            d = jnp.sum(bm * cm, axis=1, keepdims=True)
            sel = (c >= N_SSM_HEADS + 8 * g) & (c < N_SSM_HEADS + 8 * (g + 1))
            cbd = jnp.where(sel, d, cbd)
        parts = _expand_parts([jnp.exp(cf), dt * jnp.exp(tot - cf), cbd * dt + dskip_ref[...]], "ffb")

        def pair_out(pc, ms, yr):
            xp = xc_ref[rows, pc]
            rhs = jnp.concatenate([xp * keep_lo, xp * keep_hi], axis=0)
            ybuf[rows, pc] = _dot(jnp.concatenate(ms, axis=1), rhs) + yr

        pending = []
        for g in range(N_SSM_GROUPS):
            gc = slice(g * GROUP_COLS, (g + 1) * GROUP_COLS)
            bm = xc_ref[rows, D_INNER + g * D_STATE:D_INNER + (g + 1) * D_STATE]
            cm = xc_ref[rows, D_INNER + 512 + g * D_STATE:D_INNER + 512 + (g + 1) * D_STATE]
            cbm = _dot_nt(cm, bm).astype(BF16)
            ex = _dot(parts, ee_ref[:, gc])
            hg = hstate[g]
            xg = xc_ref[rows, gc]
            yrest = (_dot(cm, hg.astype(BF16)) * ex[0:BLK]
                     + yoffb_ref[rows, gc].astype(F32)
                     + xg.astype(F32) * ex[2 * BLK:3 * BLK])
            xs = xg * ex[BLK:2 * BLK].astype(BF16)
            hstate[g] = hg * ex[BLK - 1:BLK] + _dot_tn(bm, xs)
            for pr in range(4):
                ms = []
                for t in range(2):
                    h = g * 8 + pr * 2 + t
                    hb = N_SSM_HEADS + h
                    dmat = jnp.where(lower, cf2[:, h:h + 1] - src_t[h:h + 1, :],
                                     rb2[:, hb:hb + 1] - src_t[hb:hb + 1, :])
                    ms.append(cbm * jnp.exp2(dmat).astype(BF16))
                pc = slice(g * GROUP_COLS + pr * 128, g * GROUP_COLS + (pr + 1) * 128)
                pending.append((pc, ms, yrest[:, pr * 128:(pr + 1) * 128]))
                if len(pending) > SSD_PIPE_DEPTH:
                    pair_out(*pending.pop(0))
        for item in pending:
            pair_out(*item)

    def finish(ci):
        rows = slice(ci * BLK, (ci + 1) * BLK)
        zf = z_ref[rows, :].astype(F32)
        y = ybuf[rows, :] * (zf * (1.0 / (1.0 + jnp.exp(-zf))))
        scale = lax.rsqrt(jnp.mean(y * y, axis=-1, keepdims=True) + GATED_NORM_EPS)
        o = _dot((y * gnw_ref[...]).astype(BF16), wout_ref[...]) * scale
        out_ref[rows, :] = xres_ref[rows, :] + _rms(o, pw_ref[...], NORM_EPS)

    for ci in range(nch):
        chunk(ci)
        if ci > 0:
            finish(ci - 1)
    finish(nch - 1)


def _ssd_fwd_call(xc, dtraw, z, yoffb, xres, bias, alog, dskip, gnw, wout, pw, ee):
    b, l, d = xres.shape
    ts = _pick(l, SSD_TILE)
    own = lambda n: pl.BlockSpec((None, ts, n), lambda bi, i: (bi, i, 0))
    full = lambda a: pl.BlockSpec(a.shape, lambda bi, i: (0,) * a.ndim)
    return pl.pallas_call(
        functools.partial(_ssd_fwd_kernel, nch=ts // BLK),
        grid=(b, l // ts),
        in_specs=[own(CONV_DIM), own(DT_COLS), own(D_INNER), own(D_INNER), own(d),
                  full(bias), full(alog), full(dskip), full(gnw), _vmem_spec(), full(pw), _vmem_spec()],
        out_specs=own(d),
        out_shape=jax.ShapeDtypeStruct((b, l, d), F32),
        scratch_shapes=[pltpu.VMEM((N_SSM_GROUPS, D_STATE, GROUP_COLS), F32),
                        pltpu.VMEM((ts, D_INNER), F32)],
        compiler_params=_cparams(("parallel", "arbitrary")),
        name="ssd_fwd",
    )(xc, dtraw, z, yoffb, xres, bias, alog, dskip, gnw, wout, pw, ee)


def _rope_tables(l):
    inv_freq = 1.0 / (ROPE_THETA ** (jnp.arange(0, HEAD_DIM, 2, dtype=F32) / HEAD_DIM))
    ang = jnp.arange(l, dtype=F32)[:, None] * inv_freq[None, :]
    cos = jnp.cos(ang)
    sin = jnp.sin(ang)
    cos_t = jnp.concatenate([cos, cos, cos, cos], axis=1)
    sin_t = jnp.concatenate([-sin, sin, -sin, sin], axis=1)
    return cos_t, sin_t, cos.T, sin.T


def _expand_matrix():
    rows = jnp.arange(DT_COLS)
    heads = jnp.arange(D_INNER) // HEAD_DIM
    hit = ((rows[:, None] % N_SSM_HEADS) == heads[None, :]) & (rows[:, None] < 2 * N_SSM_HEADS)
    return hit.astype(BF16)


def _prep(attn_w_qkv, attn_w_o, attn_sink, ssm_w_in, ssm_conv_w, ssm_conv_b, ssm_dt_bias,
          ssm_a_log, ssm_d, ssm_norm_w, ssm_w_out, norm_mix_pre, norm_mix_post,
          norm_ffn_pre, norm_ffn_post, mlp_w_up, mlp_w_down):
    p = {}
    wq = attn_w_qkv[0]
    nq = N_Q_HEADS * HEAD_DIM
    nk = N_KV_HEADS * HEAD_DIM
    p["w_qv_t"] = jnp.concatenate([wq[:, :nq], wq[:, nq + nk:]], axis=1).T.astype(BF16)
    p["w_k"] = wq[:, nq:nq + nk].astype(BF16)
    p["w_o"] = attn_w_o[0].astype(BF16)
    p["sink"] = attn_sink[0].astype(F32)
    w_in = ssm_w_in[0]
    p["w_in"] = jnp.concatenate([w_in, jnp.zeros((D_MODEL, DT_COLS - 2 * N_SSM_HEADS), F32)], axis=1).astype(BF16)
    p["conv_w"] = ssm_conv_w[0].astype(F32)
    p["conv_b"] = ssm_conv_b[0].astype(F32).reshape(1, CONV_DIM)
    pad = jnp.zeros((DT_COLS - 2 * N_SSM_HEADS,), F32)
    p["dt_bias"] = jnp.concatenate([ssm_dt_bias[0].astype(F32).reshape(-1), pad]).reshape(1, DT_COLS)
    p["a_log"] = jnp.concatenate([ssm_a_log[0].astype(F32).reshape(-1), pad]).reshape(1, DT_COLS)
    p["d_skip"] = jnp.concatenate([jnp.zeros((N_SSM_HEADS,), F32), ssm_d[0].astype(F32), pad]).reshape(1, DT_COLS)
    p["gnw"] = ssm_norm_w[0].astype(F32).reshape(1, D_INNER)
    p["w_out"] = ssm_w_out[0].astype(BF16)
    p["mix_pre"] = norm_mix_pre.astype(F32).reshape(-1, 1, D_MODEL)
    p["mix_post"] = norm_mix_post.astype(F32).reshape(-1, 1, D_MODEL)
    p["ffn_pre"] = norm_ffn_pre.astype(F32).reshape(-1, 1, D_MODEL)
    p["ffn_post"] = norm_ffn_post.astype(F32).reshape(-1, 1, D_MODEL)
    p["w_up"] = mlp_w_up.astype(BF16)
    p["w_down"] = mlp_w_down.astype(BF16)
    p["ee"] = _expand_matrix()
    return p


def _pick(n, pref):
    t = pref
    while n % t:
        t //= 2
    return t


def _trunk(x, p, rope):
    b, l, d = x.shape
    t = b * l
    tm = _pick(t, FFN_TILE)
    tq = _pick(l, ATTN_TILE)
    cos_t, sin_t, cos_tt, sin_tt = rope
    q_t, k, v_t = _qkv_call(x, p["mix_pre"][0], p["w_qv_t"], p["w_k"], cos_t, sin_t, cos_tt, sin_tt,
                            _pick(l, 512))
    x = _attn_call(p["sink"], q_t, k, v_t, x, p["w_o"], p["mix_post"][0], tq)
    x = _ffn_call(x.reshape(t, d), p["ffn_pre"][0], p["w_up"][0], p["w_down"][0], p["ffn_post"][0], tm)
    x = x.reshape(b, l, d)
    z, xc, dtraw = _inproj_call(x, p["mix_pre"][1], p["w_in"], p["conv_w"], p["conv_b"], _pick(l, 512))
    yoffb = _ssd_bwd_call(xc, dtraw, p["dt_bias"], p["a_log"], p["ee"])
    x = _ssd_fwd_call(xc, dtraw, z, yoffb, x,
                      p["dt_bias"], p["a_log"], p["d_skip"], p["gnw"], p["w_out"], p["mix_post"][1],
                      p["ee"])
    x = _ffn_call(x.reshape(t, d), p["ffn_pre"][1], p["w_up"][1], p["w_down"][1], p["ffn_post"][1], tm)
    return x.reshape(b, l, d)


def kernel(x_prompt, x_sample, attn_w_qkv, attn_w_o, attn_sink, ssm_w_in, ssm_conv_w, ssm_conv_b,
           ssm_dt_bias, ssm_a_log, ssm_d, ssm_norm_w, ssm_w_out, norm_mix_pre, norm_mix_post,
           norm_ffn_pre, norm_ffn_post, mlp_w_up, mlp_w_down):
    p = _prep(attn_w_qkv, attn_w_o, attn_sink, ssm_w_in, ssm_conv_w, ssm_conv_b, ssm_dt_bias,
              ssm_a_log, ssm_d, ssm_norm_w, ssm_w_out, norm_mix_pre, norm_mix_post,
              norm_ffn_pre, norm_ffn_post, mlp_w_up, mlp_w_down)
    rope = _rope_tables(max(x_prompt.shape[1], x_sample.shape[1]))
    return (_trunk(x_prompt, p, rope), _trunk(x_sample, p, rope))
```

```python
import functools
import math

import jax
import jax.numpy as jnp
from jax import lax
from jax.experimental import pallas as pl
from jax.experimental.pallas import tpu as pltpu

F32 = jnp.float32
BF16 = jnp.bfloat16

D_MODEL = 1024
HEAD_DIM = 64
N_Q_HEADS = 16
N_KV_HEADS = 4
Q_PER_KV = 4
WINDOW = 128
BLK = 128
ROPE_THETA = 10000.0
D_INNER = 2048
N_SSM_HEADS = 32
N_SSM_GROUPS = 4
D_STATE = 128
D_CONV = 5
CONV_PAD = 2
CONV_DIM = D_INNER + 2 * N_SSM_GROUPS * D_STATE
GROUP_COLS = D_INNER // N_SSM_GROUPS
D_FF = 4096
NORM_EPS = 1e-6
GATED_NORM_EPS = 1e-5
LOG2E = 1.4426950408889634
NEG_BIG = -1e30

VMEM_LIMIT = 56 * 1024 * 1024


def _cparams(sem):
    return pltpu.CompilerParams(dimension_semantics=sem, vmem_limit_bytes=VMEM_LIMIT)


def _vmem_spec():
    return pl.BlockSpec(memory_space=pltpu.VMEM)


def _rms(x, w, eps):
    ms = jnp.mean(x * x, axis=-1, keepdims=True)
    return x * lax.rsqrt(ms + eps) * w


def _dot(a, b):
    return jnp.dot(a, b, preferred_element_type=F32)


def _dot_nt(a, b):
    return lax.dot_general(a, b, (((1,), (1,)), ((), ())), preferred_element_type=F32)


def _dot_tn(a, b):
    return lax.dot_general(a, b, (((0,), (0,)), ((), ())), preferred_element_type=F32)


K_COLS = N_KV_HEADS * HEAD_DIM
HALF = HEAD_DIM // 2


def _qkv_kernel(x_ref, nw_ref, wqv_ref, wk_ref, cos_ref, sin_ref, cost_ref, sint_ref, q_ref, k_ref, v_ref):
    h = _rms(x_ref[...], nw_ref[...], NORM_EPS).astype(BF16)
    yk = _dot(h, wk_ref[...])
    cos = cos_ref[...]
    sin = sin_ref[...]
    lane = lax.broadcasted_iota(jnp.int32, cos.shape, 1)
    first_half = (lane & HALF) == 0
    for s in range(K_COLS // 128):
        yc = yk[:, s * 128:(s + 1) * 128]
        rot = jnp.where(first_half, pltpu.roll(yc, 128 - HALF, 1), pltpu.roll(yc, HALF, 1))
        k_ref[:, s * 128:(s + 1) * 128] = (yc * cos + rot * sin).astype(BF16)
    cost = cost_ref[...]
    sint = sint_ref[...]
    qscale = HEAD_DIM ** -0.5 * LOG2E
    yqv = _dot_nt(wqv_ref[...], h)
    for hd in range(N_Q_HEADS):
        t1 = yqv[hd * HEAD_DIM:hd * HEAD_DIM + HALF]
        t2 = yqv[hd * HEAD_DIM + HALF:(hd + 1) * HEAD_DIM]
        q_ref[hd * HEAD_DIM:hd * HEAD_DIM + HALF, :] = ((t1 * cost - t2 * sint) * qscale).astype(BF16)
        q_ref[hd * HEAD_DIM + HALF:(hd + 1) * HEAD_DIM, :] = ((t2 * cost + t1 * sint) * qscale).astype(BF16)
    ones = jnp.ones((HEAD_DIM, h.shape[0]), BF16)
    for g in range(N_KV_HEADS):
        lo = N_Q_HEADS * HEAD_DIM + g * HEAD_DIM
        v_ref[g * 128:g * 128 + HEAD_DIM, :] = yqv[lo:lo + HEAD_DIM].astype(BF16)
        v_ref[g * 128 + HEAD_DIM:(g + 1) * 128, :] = ones


def _qkv_call(x, nw, wqv_t, wk, cos_t, sin_t, cos_tt, sin_tt, tm):
    b, l, d = x.shape
    grid = (b, l // tm)
    tok = lambda n: pl.BlockSpec((None, tm, n), lambda bi, i: (bi, i, 0))
    chan = lambda n: pl.BlockSpec((None, n, tm), lambda bi, i: (bi, 0, i))
    return pl.pallas_call(
        _qkv_kernel,
        grid=grid,
        in_specs=[tok(d),
                  pl.BlockSpec((1, d), lambda bi, i: (0, 0)),
                  _vmem_spec(), _vmem_spec(),
                  pl.BlockSpec((tm, 128), lambda bi, i: (i, 0)),
                  pl.BlockSpec((tm, 128), lambda bi, i: (i, 0)),
                  pl.BlockSpec((HALF, tm), lambda bi, i: (0, i)),
                  pl.BlockSpec((HALF, tm), lambda bi, i: (0, i))],
        out_specs=[chan(1024), tok(K_COLS), chan(512)],
        out_shape=[jax.ShapeDtypeStruct((b, 1024, l), BF16),
                   jax.ShapeDtypeStruct((b, l, K_COLS), BF16),
                   jax.ShapeDtypeStruct((b, 512, l), BF16)],
        compiler_params=_cparams(("parallel", "parallel")),
        name="qkv_rope",
    )(x, nw, wqv_t, wk, cos_t, sin_t, cos_tt, sin_tt)


ATTN_PIPE_DEPTH = 2
ATTN_TILE = 1024


def _attn_kernel(sink_ref, q_ref, kp_ref, ko_ref, kn_ref, vp_ref, vo_ref, vn_ref, x_ref,
                 wo_ref, pw_ref, out_ref, kbuf, vbuf, obuf, *, tq):
    i = pl.program_id(1)
    nsteps = pl.num_programs(1)
    nsub = tq // BLK
    kbuf[0:BLK, :] = kp_ref[...]
    kbuf[BLK:BLK + tq, :] = ko_ref[...]
    kbuf[BLK + tq:, :] = kn_ref[...]
    vbuf[:, 0:BLK] = vp_ref[...]
    vbuf[:, BLK:BLK + tq] = vo_ref[...]
    vbuf[:, BLK + tq:] = vn_ref[...]

    c = lax.broadcasted_iota(jnp.int32, (3 * BLK, BLK), 0)
    r = lax.broadcasted_iota(jnp.int32, (3 * BLK, BLK), 1)
    er = lax.broadcasted_iota(jnp.int32, (BLK, BLK), 0)
    ec = lax.broadcasted_iota(jnp.int32, (BLK, BLK), 1)
    eye = jnp.where(er == ec, 1.0, 0.0).astype(BF16)
    eye4 = jnp.concatenate([eye] * Q_PER_KV, axis=1)
    zero_half = jnp.zeros((HEAD_DIM, BLK), BF16)
    first_lo = jnp.where(i > 0, 0, BLK)
    last_hi = jnp.where(i < nsteps - 1, 3 * BLK, 2 * BLK)

    def scores(j, g, bias):
        qcols = slice(j * BLK, (j + 1) * BLK)
        kk = kbuf[j * BLK:(j + 3) * BLK, (g // 2) * 128:(g // 2 + 1) * 128]
        blocks = []
        for t in range(Q_PER_KV):
            hd = g * Q_PER_KV + t
            qh = q_ref[hd * HEAD_DIM:(hd + 1) * HEAD_DIM, qcols]
            blocks.append(jnp.concatenate([qh, zero_half] if g % 2 == 0 else [zero_half, qh], axis=0))
        rhs = jnp.concatenate([jnp.concatenate(blocks, axis=1), eye4], axis=0)
        s = _dot(jnp.concatenate([kk, bias], axis=1), rhs)
        sink = jnp.concatenate(
            [jnp.full((1, BLK), sink_ref[g * Q_PER_KV + t] * LOG2E, F32) for t in range(Q_PER_KV)], axis=1)
        m = jnp.maximum(jnp.max(s, axis=0, keepdims=True), sink)
        return s, m, sink

    def finish(j, g, s, m, sink):
        qcols = slice(j * BLK, (j + 1) * BLK)
        p = jnp.exp2(s - m).astype(BF16)
        o = _dot(vbuf[g * 128:(g + 1) * 128, j * BLK:(j + 3) * BLK], p)
        den = o[HEAD_DIM:] + jnp.exp2(sink - m)
        on = (o[:HEAD_DIM] * (1.0 / den)).astype(BF16)
        for t in range(Q_PER_KV):
            hd = g * Q_PER_KV + t
            obuf[hd * HEAD_DIM:(hd + 1) * HEAD_DIM, qcols] = on[:, t * BLK:(t + 1) * BLK]

    pending = []
    for j in range(nsub):
        cmin = r
        cmax = r + 2 * WINDOW
        if j == 0:
            cmin = jnp.maximum(cmin, first_lo)
        if j == nsub - 1:
            cmax = jnp.minimum(cmax, last_hi - 1)
        bias = jnp.where(c >= cmin, jnp.where(c <= cmax, 0.0, NEG_BIG), NEG_BIG).astype(BF16)
        for g in range(N_KV_HEADS):
            pending.append((j, g) + scores(j, g, bias))
            if len(pending) > ATTN_PIPE_DEPTH:
                finish(*pending.pop(0))
    for item in pending:
        finish(*item)
    a = _dot_tn(obuf[...], wo_ref[...])
    out_ref[...] = x_ref[...] + _rms(a, pw_ref[...], NORM_EPS)


def _attn_call(sink, q_t, k, v_t, x, wo, pw, tq):
    b, l, d = x.shape
    nb = l // BLK
    rr = tq // BLK
    grid = (b, l // tq)
    tok = lambda n: pl.BlockSpec((None, tq, n), lambda bi, i: (bi, i, 0))
    chan = lambda n: pl.BlockSpec((None, n, tq), lambda bi, i: (bi, 0, i))
    kprev = pl.BlockSpec((None, BLK, K_COLS), lambda bi, i: (bi, jnp.maximum(i * rr - 1, 0), 0))
    knext = pl.BlockSpec((None, BLK, K_COLS), lambda bi, i: (bi, jnp.minimum((i + 1) * rr, nb - 1), 0))
    vprev = pl.BlockSpec((None, 512, BLK), lambda bi, i: (bi, 0, jnp.maximum(i * rr - 1, 0)))
    vnext = pl.BlockSpec((None, 512, BLK), lambda bi, i: (bi, 0, jnp.minimum((i + 1) * rr, nb - 1)))
    return pl.pallas_call(
        functools.partial(_attn_kernel, tq=tq),
        grid=grid,
        in_specs=[pl.BlockSpec(memory_space=pltpu.SMEM),
                  chan(1024), kprev, tok(K_COLS), knext, vprev, chan(512), vnext, tok(d),
                  _vmem_spec(),
                  pl.BlockSpec((1, d), lambda bi, i: (0, 0))],
        out_specs=tok(d),
        out_shape=jax.ShapeDtypeStruct((b, l, d), F32),
        scratch_shapes=[pltpu.VMEM((tq + 2 * BLK, K_COLS), BF16),
                        pltpu.VMEM((512, tq + 2 * BLK), BF16),
                        pltpu.VMEM((1024, tq), BF16)],
        compiler_params=_cparams(("parallel", "parallel")),
        name="attn_block",
    )(sink, q_t, k, k, k, v_t, v_t, v_t, x, wo, pw)


FF_CHUNK = 1024
FFN_TILE = 1024


def _ffn_kernel(x_ref, nw_ref, wu_ref, wd_ref, pw_ref, out_ref):
    x = x_ref[...]
    h = _rms(x, nw_ref[...], NORM_EPS).astype(BF16)
    acc = None
    for c in range(D_FF // FF_CHUNK):
        u = _dot(h, wu_ref[:, c * FF_CHUNK:(c + 1) * FF_CHUNK])
        a = jnp.square(jnp.maximum(u, 0.0)).astype(BF16)
        part = _dot(a, wd_ref[c * FF_CHUNK:(c + 1) * FF_CHUNK, :])
        acc = part if acc is None else acc + part
    out_ref[...] = x + _rms(acc, pw_ref[...], NORM_EPS)


def _ffn_call(x2d, nw, wu, wd, pw, tm):
    t, d = x2d.shape
    vec = pl.BlockSpec((1, d), lambda i: (0, 0))
    return pl.pallas_call(
        _ffn_kernel,
        grid=(t // tm,),
        in_specs=[pl.BlockSpec((tm, d), lambda i: (i, 0)), vec, _vmem_spec(), _vmem_spec(), vec],
        out_specs=pl.BlockSpec((tm, d), lambda i: (i, 0)),
        out_shape=jax.ShapeDtypeStruct((t, d), F32),
        compiler_params=_cparams(("parallel",)),
        name="ffn",
    )(x2d, nw, wu, wd, pw)


DT_COLS = 128
IN_EXT = D_INNER + CONV_DIM + DT_COLS


CONV_HALO = 8
CONV_STRIDE = 4
CONV_COLS = 512


def _inproj_kernel(xo_ref, xp_ref, xn_ref, nw_ref, w_ref, cw_ref, cb_ref, z_ref, xc_ref, dt_ref,
                   cbuf, sbuf, *, tm):
    i = pl.program_id(1)
    nsteps = pl.num_programs(1)
    x_ext = jnp.concatenate([xo_ref[...], xp_ref[...], xn_ref[...]], axis=0)
    h_ext = _rms(x_ext, nw_ref[...], NORM_EPS).astype(BF16)
    h = h_ext[0:tm]

    def gate_proj(c):
        zz = _dot(h, w_ref[:, c * 512:(c + 1) * 512])
        z_ref[:, c * 512:(c + 1) * 512] = (zz * (1.0 / (1.0 + jnp.exp(-zz)))).astype(BF16)

    keep_prev = jnp.where(i > 0, 1.0, 0.0)
    keep_next = jnp.where(i < nsteps - 1, 1.0, 0.0)
    slabs = CONV_COLS // 128

    def project(cb):
        lo = D_INNER + cb * CONV_COLS
        y = _dot(h_ext, w_ref[:, lo:lo + CONV_COLS])
        for s in range(slabs):
            cs = cb * slabs + s
            cols = slice(s * 128, (s + 1) * 128)
            cbuf[cs, 0:CONV_HALO, :] = y[tm:tm + CONV_HALO, cols] * keep_prev
            cbuf[cs, CONV_HALO:CONV_HALO + tm, :] = y[0:tm, cols]
            cbuf[cs, CONV_HALO + tm:, :] = y[tm + CONV_HALO:, cols] * keep_next

    def conv(cb):
        for s in range(slabs):
            cs = cb * slabs + s
            cols = slice(cs * 128, (cs + 1) * 128)
            wk = [jnp.broadcast_to(cw_ref[k:k + 1, cols], (8, 128)) for k in range(D_CONV)]
            bk = jnp.broadcast_to(cb_ref[:, cols], (8, 128))
            for base in range(0, tm, 8 * CONV_STRIDE):
                for rr in range(CONV_STRIDE):
                    row = base + rr
                    acc = bk
                    for k in range(D_CONV):
                        acc = acc + wk[k] * cbuf[cs, pl.ds(CONV_HALO - CONV_PAD + k + row, 8, stride=CONV_STRIDE), :]
                    sbuf[cs, pl.ds(row, 8, stride=CONV_STRIDE), :] = acc * (1.0 / (1.0 + jnp.exp(-acc)))
            xc_ref[:, cols] = sbuf[cs].astype(BF16)

    nblocks = CONV_DIM // CONV_COLS
    project(0)
    for cb in range(nblocks):
        if cb + 1 < nblocks:
            project(cb + 1)
        conv(cb)
        if cb < D_INNER // 512:
            gate_proj(cb)
    dt_ref[...] = _dot(h, w_ref[:, D_INNER + CONV_DIM:])


def _inproj_call(x, nw, w_ext, cw, cb, tm):
    b, l, d = x.shape
    nh = l // CONV_HALO
    hpt = tm // CONV_HALO
    tok = lambda n: pl.BlockSpec((None, tm, n), lambda bi, i: (bi, i, 0))
    prev = pl.BlockSpec((None, CONV_HALO, d), lambda bi, i: (bi, jnp.maximum(i * hpt - 1, 0), 0))
    nxt = pl.BlockSpec((None, CONV_HALO, d), lambda bi, i: (bi, jnp.minimum((i + 1) * hpt, nh - 1), 0))
    full = lambda a: pl.BlockSpec(a.shape, lambda bi, i: (0,) * a.ndim)
    return pl.pallas_call(
        functools.partial(_inproj_kernel, tm=tm),
        grid=(b, l // tm),
        in_specs=[tok(d), prev, nxt, pl.BlockSpec((1, d), lambda bi, i: (0, 0)), _vmem_spec(), full(cw), full(cb)],
        out_specs=[tok(D_INNER), tok(CONV_DIM), tok(DT_COLS)],
        out_shape=[jax.ShapeDtypeStruct((b, l, D_INNER), BF16),
                   jax.ShapeDtypeStruct((b, l, CONV_DIM), BF16),
                   jax.ShapeDtypeStruct((b, l, DT_COLS), F32)],
        scratch_shapes=[pltpu.VMEM((CONV_DIM // 128, tm + 2 * CONV_HALO, 128), F32),
                        pltpu.VMEM((CONV_DIM // 128, tm, 128), F32)],
        compiler_params=_cparams(("parallel", "parallel")),
        name="in_proj",
    )(x, x, x, nw, w_ext, cw, cb)


def _softplus(x):
    return jnp.maximum(x, 0.0) + jnp.log1p(jnp.exp(-jnp.abs(x)))


def _pieces(v):
    hi = v.astype(BF16)
    r1 = v - hi.astype(F32)
    mid = r1.astype(BF16)
    lo = (r1 - mid.astype(F32)).astype(BF16)
    return hi, mid, lo


def _tri_cumsum(tri, v):
    hi, mid, lo = _pieces(v)
    return _dot(tri, hi) + _dot(tri, mid) + _dot(tri, lo)


def _tri_cumsum_stacked(tri, v):
    st = _dot(tri, jnp.concatenate(_pieces(v), axis=1))
    return st[:, 0:BLK] + st[:, BLK:2 * BLK] + st[:, 2 * BLK:3 * BLK]


def _expand_parts(vs, kinds):
    lane = lax.broadcasted_iota(jnp.int32, (BLK, DT_COLS), 1)
    first = lane < N_SSM_HEADS
    parts = []
    for v, kind in zip(vs, kinds):
        hi = v.astype(BF16).astype(F32)
        lo = v - hi
        if kind == "f":
            comb = jnp.where(first, hi, pltpu.roll(lo, N_SSM_HEADS, 1))
        else:
            comb = jnp.where(first, pltpu.roll(lo, DT_COLS - N_SSM_HEADS, 1), hi)
        parts.append(comb.astype(BF16))
    return jnp.concatenate(parts, axis=0)


def _dt_and_decay(dtraw_ref, rows, bias_ref, alog_ref):
    dt = _softplus(dtraw_ref[rows, :] + bias_ref[...])
    lane = lax.broadcasted_iota(jnp.int32, (1, DT_COLS), 1)
    a = jnp.where(lane < 2 * N_SSM_HEADS, -jnp.exp(alog_ref[...]), 0.0)
    return dt, dt * a


def _tri(lower):
    r = lax.broadcasted_iota(jnp.int32, (BLK, BLK), 0)
    c = lax.broadcasted_iota(jnp.int32, (BLK, BLK), 1)
    m = (c <= r) if lower else (c >= r)
    return jnp.where(m, 1.0, 0.0).astype(BF16)


SSD_TILE = 512
SSD_PIPE_DEPTH = 2


def _ssd_bwd_kernel(xc_ref, dtraw_ref, bias_ref, alog_ref, ee_ref, yoff_ref, hstate, *, nch):
    i = pl.program_id(1)

    @pl.when(i == 0)
    def _():
        hstate[...] = jnp.zeros_like(hstate)

    def chunk(it, carry):
        rows = pl.ds(pl.multiple_of((nch - 1 - it) * BLK, BLK), BLK)
        dt, da = _dt_and_decay(dtraw_ref, rows, bias_ref, alog_ref)
        rb = _tri_cumsum_stacked(_tri(False), da)
        tot = rb[0:1, :]
        ex = _dot(_expand_parts([jnp.exp(rb), dt * jnp.exp(tot - rb)], "bb"), ee_ref[...])
        for g in range(N_SSM_GROUPS):
            gc = slice(g * GROUP_COLS, (g + 1) * GROUP_COLS)
            bm = xc_ref[rows, D_INNER + g * D_STATE:D_INNER + (g + 1) * D_STATE]
            cm = xc_ref[rows, D_INNER + 512 + g * D_STATE:D_INNER + 512 + (g + 1) * D_STATE]
            hg = hstate[g]
            yoff_ref[rows, gc] = (_dot(cm, hg.astype(BF16)) * ex[0:BLK, gc]).astype(BF16)
            xs = xc_ref[rows, gc] * ex[BLK:2 * BLK, gc].astype(BF16)
            hstate[g] = hg * ex[0:1, gc] + _dot_tn(bm, xs)
        return carry

    lax.fori_loop(0, nch, chunk, 0)


def _ssd_bwd_call(xc, dtraw, bias, alog, ee):
    b, l, _ = xc.shape
    ts = _pick(l, SSD_TILE)
    nt = l // ts
    own = lambda n: pl.BlockSpec((None, ts, n), lambda bi, i: (bi, nt - 1 - i, 0))
    full = lambda a: pl.BlockSpec(a.shape, lambda bi, i: (0,) * a.ndim)
    return pl.pallas_call(
        functools.partial(_ssd_bwd_kernel, nch=ts // BLK),
        grid=(b, nt),
        in_specs=[own(CONV_DIM), own(DT_COLS), full(bias), full(alog), _vmem_spec()],
        out_specs=own(D_INNER),
        out_shape=jax.ShapeDtypeStruct((b, l, D_INNER), BF16),
        scratch_shapes=[pltpu.VMEM((N_SSM_GROUPS, D_STATE, GROUP_COLS), F32)],
        compiler_params=_cparams(("parallel", "arbitrary")),
        name="ssd_bwd",
    )(xc, dtraw, bias, alog, ee)


def _ssd_fwd_kernel(xc_ref, dtraw_ref, z_ref, yoffb_ref, xres_ref, bias_ref, alog_ref, dskip_ref,
                    gnw_ref, wout_ref, pw_ref, ee_ref, out_ref, hstate, ybuf, *, nch):
    i = pl.program_id(1)

    @pl.when(i == 0)
    def _():
        hstate[...] = jnp.zeros_like(hstate)

    def chunk(ci):
        rows = slice(ci * BLK, (ci + 1) * BLK)
        dt, da = _dt_and_decay(dtraw_ref, rows, bias_ref, alog_ref)
        cf = _tri_cumsum(_tri(True), da)
        rb = _tri_cumsum(_tri(False), da)
        tot = cf[BLK - 1:BLK, :]
        cf2 = cf * LOG2E
        rb2 = rb * LOG2E
        r = lax.broadcasted_iota(jnp.int32, (BLK, BLK), 0)
        c = lax.broadcasted_iota(jnp.int32, (BLK, BLK), 1)
        src_t = jnp.where(r < N_SSM_HEADS, cf2.T, rb2.T) - jnp.log2(dt.T)
        lower = c <= r
        keep_lo = jnp.where(c < HEAD_DIM, 1.0, 0.0).astype(BF16)
        keep_hi = jnp.where(c < HEAD_DIM, 0.0, 1.0).astype(BF16)

        cbd = jnp.zeros((BLK, DT_COLS), F32)
        for g in range(N_SSM_GROUPS):
            bm = xc_ref[rows, D_INNER + g * D_STATE:D_INNER + (g + 1) * D_STATE].astype(F32)
            cm = xc_ref[rows, D_INNER + 512 + g * D_STATE:D_INNER + 512 + (g + 1) * D_STATE].astype(F32)
            d = jnp.sum(bm * cm, axis=1, keepdims=True)
            sel = (c >= N_SSM_HEADS + 8 * g) & (c < N_SSM_HEADS + 8 * (g + 1))
            cbd = jnp.where(sel, d, cbd)
        parts = _expand_parts([jnp.exp(cf), dt * jnp.exp(tot - cf), cbd * dt + dskip_ref[...]], "ffb")

        def pair_out(pc, ms, yr):
            xp = xc_ref[rows, pc]
            rhs = jnp.concatenate([xp * keep_lo, xp * keep_hi], axis=0)
            ybuf[rows, pc] = _dot(jnp.concatenate(ms, axis=1), rhs) + yr

        pending = []
        for g in range(N_SSM_GROUPS):
            gc = slice(g * GROUP_COLS, (g + 1) * GROUP_COLS)
            bm = xc_ref[rows, D_INNER + g * D_STATE:D_INNER + (g + 1) * D_STATE]
            cm = xc_ref[rows, D_INNER + 512 + g * D_STATE:D_INNER + 512 + (g + 1) * D_STATE]
            cbm = _dot_nt(cm, bm).astype(BF16)
            ex = _dot(parts, ee_ref[:, gc])
            hg = hstate[g]
            xg = xc_ref[rows, gc]
            yrest = (_dot(cm, hg.astype(BF16)) * ex[0:BLK]
                     + yoffb_ref[rows, gc].astype(F32)
                     + xg.astype(F32) * ex[2 * BLK:3 * BLK])
            xs = xg * ex[BLK:2 * BLK].astype(BF16)
            hstate[g] = hg * ex[BLK - 1:BLK] + _dot_tn(bm, xs)
            for pr in range(4):
                ms = []
                for t in range(2):
                    h = g * 8 + pr * 2 + t
                    hb = N_SSM_HEADS + h
                    dmat = jnp.where(lower, cf2[:, h:h + 1] - src_t[h:h + 1, :],
                                     rb2[:, hb:hb + 1] - src_t[hb:hb + 1, :])
                    ms.append(cbm * jnp.exp2(dmat).astype(BF16))
                pc = slice(g * GROUP_COLS + pr * 128, g * GROUP_COLS + (pr + 1) * 128)
                pending.append((pc, ms, yrest[:, pr * 128:(pr + 1) * 128]))
                if len(pending) > SSD_PIPE_DEPTH:
                    pair_out(*pending.pop(0))
        for item in pending:
            pair_out(*item)

    def finish(ci):
        rows = slice(ci * BLK, (ci + 1) * BLK)
        y = ybuf[rows, :] * z_ref[rows, :].astype(F32)
        scale = lax.rsqrt(jnp.mean(y * y, axis=-1, keepdims=True) + GATED_NORM_EPS)
        o = _dot((y * gnw_ref[...]).astype(BF16), wout_ref[...]) * scale
        out_ref[rows, :] = xres_ref[rows, :] + _rms(o, pw_ref[...], NORM_EPS)

    for ci in range(nch):
        chunk(ci)
        if ci > 0:
            finish(ci - 1)
    finish(nch - 1)


def _ssd_fwd_call(xc, dtraw, z, yoffb, xres, bias, alog, dskip, gnw, wout, pw, ee):
    b, l, d = xres.shape
    ts = _pick(l, SSD_TILE)
    own = lambda n: pl.BlockSpec((None, ts, n), lambda bi, i: (bi, i, 0))
    full = lambda a: pl.BlockSpec(a.shape, lambda bi, i: (0,) * a.ndim)
    return pl.pallas_call(
        functools.partial(_ssd_fwd_kernel, nch=ts // BLK),
        grid=(b, l // ts),
        in_specs=[own(CONV_DIM), own(DT_COLS), own(D_INNER), own(D_INNER), own(d),
                  full(bias), full(alog), full(dskip), full(gnw), _vmem_spec(), full(pw), _vmem_spec()],
        out_specs=own(d),
        out_shape=jax.ShapeDtypeStruct((b, l, d), F32),
        scratch_shapes=[pltpu.VMEM((N_SSM_GROUPS, D_STATE, GROUP_COLS), F32),
                        pltpu.VMEM((ts, D_INNER), F32)],
        compiler_params=_cparams(("parallel", "arbitrary")),
        name="ssd_fwd",
    )(xc, dtraw, z, yoffb, xres, bias, alog, dskip, gnw, wout, pw, ee)


def _rope_tables(l):
    inv_freq = 1.0 / (ROPE_THETA ** (jnp.arange(0, HEAD_DIM, 2, dtype=F32) / HEAD_DIM))
    ang = jnp.arange(l, dtype=F32)[:, None] * inv_freq[None, :]
    cos = jnp.cos(ang)
    sin = jnp.sin(ang)
    cos_t = jnp.concatenate([cos, cos, cos, cos], axis=1)
    sin_t = jnp.concatenate([-sin, sin, -sin, sin], axis=1)
    return cos_t, sin_t, cos.T, sin.T


def _expand_matrix():
    rows = jnp.arange(DT_COLS)
    heads = jnp.arange(D_INNER) // HEAD_DIM
    hit = ((rows[:, None] % N_SSM_HEADS) == heads[None, :]) & (rows[:, None] < 2 * N_SSM_HEADS)
    return hit.astype(BF16)


def _prep(attn_w_qkv, attn_w_o, attn_sink, ssm_w_in, ssm_conv_w, ssm_conv_b, ssm_dt_bias,
          ssm_a_log, ssm_d, ssm_norm_w, ssm_w_out, norm_mix_pre, norm_mix_post,
          norm_ffn_pre, norm_ffn_post, mlp_w_up, mlp_w_down):
    p = {}
    wq = attn_w_qkv[0]
    nq = N_Q_HEADS * HEAD_DIM
    nk = N_KV_HEADS * HEAD_DIM
    p["w_qv_t"] = jnp.concatenate([wq[:, :nq], wq[:, nq + nk:]], axis=1).T.astype(BF16)
    p["w_k"] = wq[:, nq:nq + nk].astype(BF16)
    p["w_o"] = attn_w_o[0].astype(BF16)
    p["sink"] = attn_sink[0].astype(F32)
    w_in = ssm_w_in[0]
    p["w_in"] = jnp.concatenate([w_in, jnp.zeros((D_MODEL, DT_COLS - 2 * N_SSM_HEADS), F32)], axis=1).astype(BF16)
    p["conv_w"] = ssm_conv_w[0].astype(F32)
    p["conv_b"] = ssm_conv_b[0].astype(F32).reshape(1, CONV_DIM)
    pad = jnp.zeros((DT_COLS - 2 * N_SSM_HEADS,), F32)
    p["dt_bias"] = jnp.concatenate([ssm_dt_bias[0].astype(F32).reshape(-1), pad]).reshape(1, DT_COLS)
    p["a_log"] = jnp.concatenate([ssm_a_log[0].astype(F32).reshape(-1), pad]).reshape(1, DT_COLS)
    p["d_skip"] = jnp.concatenate([jnp.zeros((N_SSM_HEADS,), F32), ssm_d[0].astype(F32), pad]).reshape(1, DT_COLS)
    p["gnw"] = ssm_norm_w[0].astype(F32).reshape(1, D_INNER)
    p["w_out"] = ssm_w_out[0].astype(BF16)
    p["mix_pre"] = norm_mix_pre.astype(F32).reshape(-1, 1, D_MODEL)
    p["mix_post"] = norm_mix_post.astype(F32).reshape(-1, 1, D_MODEL)
    p["ffn_pre"] = norm_ffn_pre.astype(F32).reshape(-1, 1, D_MODEL)
    p["ffn_post"] = norm_ffn_post.astype(F32).reshape(-1, 1, D_MODEL)
    p["w_up"] = mlp_w_up.astype(BF16)
    p["w_down"] = mlp_w_down.astype(BF16)
    p["ee"] = _expand_matrix()
    return p


def _pick(n, pref):
    t = pref
    while n % t:
        t //= 2
    return t


def _trunk(x, p, rope):
    b, l, d = x.shape
    t = b * l
    tm = _pick(t, FFN_TILE)
    tq = _pick(l, ATTN_TILE)
    cos_t, sin_t, cos_tt, sin_tt = rope
    q_t, k, v_t = _qkv_call(x, p["mix_pre"][0], p["w_qv_t"], p["w_k"], cos_t, sin_t, cos_tt, sin_tt,
                            _pick(l, 512))
    x = _attn_call(p["sink"], q_t, k, v_t, x, p["w_o"], p["mix_post"][0], tq)
    x = _ffn_call(x.reshape(t, d), p["ffn_pre"][0], p["w_up"][0], p["w_down"][0], p["ffn_post"][0], tm)
    x = x.reshape(b, l, d)
    z, xc, dtraw = _inproj_call(x, p["mix_pre"][1], p["w_in"], p["conv_w"], p["conv_b"], _pick(l, 512))
    yoffb = _ssd_bwd_call(xc, dtraw, p["dt_bias"], p["a_log"], p["ee"])
    x = _ssd_fwd_call(xc, dtraw, z, yoffb, x,
                      p["dt_bias"], p["a_log"], p["d_skip"], p["gnw"], p["w_out"], p["mix_post"][1],
                      p["ee"])
    x = _ffn_call(x.reshape(t, d), p["ffn_pre"][1], p["w_up"][1], p["w_down"][1], p["ffn_post"][1], tm)
    return x.reshape(b, l, d)


def kernel(x_prompt, x_sample, attn_w_qkv, attn_w_o, attn_sink, ssm_w_in, ssm_conv_w, ssm_conv_b,
           ssm_dt_bias, ssm_a_log, ssm_d, ssm_norm_w, ssm_w_out, norm_mix_pre, norm_mix_post,
           norm_ffn_pre, norm_ffn_post, mlp_w_up, mlp_w_down):
    p = _prep(attn_w_qkv, attn_w_o, attn_sink, ssm_w_in, ssm_conv_w, ssm_conv_b, ssm_dt_bias,
              ssm_a_log, ssm_d, ssm_norm_w, ssm_w_out, norm_mix_pre, norm_mix_post,
              norm_ffn_pre, norm_ffn_post, mlp_w_up, mlp_w_down)
    rope = _rope_tables(max(x_prompt.shape[1], x_sample.shape[1]))
    return (_trunk(x_prompt, p, rope), _trunk(x_sample, p, rope))
```
